```python
import jax, jax.numpy as jnp
from jax import lax
import numpy as np

D_MODEL = 1024
BATCH = 2
SEQ = 8192
DEPTH = 2

N_MEM = 256
POOL_WINDOWS = (2, 4, 8, 16)
POOL_GROUPS = 4
POOL_W = D_MODEL // 2
POOL_GW = POOL_W // POOL_GROUPS
LRU_W = D_MODEL
LRU_HEADS = 8
LRU_HD = LRU_W // LRU_HEADS
CONV_W = 4
LRU_C = 8.0
FOX_HEADS = 8
FOX_HD = 64
FOX_W = FOX_HEADS * FOX_HD
Q_BLOCK = 128
X_HEADS = 4
X_HD = D_MODEL // X_HEADS
D_FF = ((8 * D_MODEL // 3 + 127) // 128) * 128
N_BRANCH = 3
EPS = 1e-6
IN_SIZES = (POOL_W, LRU_W, LRU_W, FOX_W, FOX_W, FOX_W, FOX_HEADS, N_BRANCH * D_MODEL)
IN_W = sum(IN_SIZES)

kernel_name = "hybrid_pool_rglru_fox_macaron_block"


def rmsnorm(x, g):
    xf = x.astype(jnp.float32)
    y = xf * lax.rsqrt(jnp.mean(xf * xf, axis=-1, keepdims=True) + EPS)
    return (y * g.astype(jnp.float32)).astype(x.dtype)


def swiglu(h, w_in, w_out):
    a, b = jnp.split(h @ w_in, 2, axis=-1)
    return (jax.nn.silu(a) * b) @ w_out


def pool_mixer(xa, w_grp, scale):
    B, S, _ = xa.shape
    xf = xa.astype(jnp.float32)
    cs = jnp.pad(jnp.cumsum(xf, axis=1), ((0, 0), (1, 0), (0, 0)))
    pos = jnp.arange(1, S + 1, dtype=jnp.float32)
    outs = []
    for g, w in enumerate(POOL_WINDOWS):
        c = cs[:, :, g * POOL_GW:(g + 1) * POOL_GW]
        lo = jnp.pad(c[:, :S + 1 - w], ((0, 0), (w - 1, 0), (0, 0)))
        cnt = jnp.minimum(pos, float(w))[None, :, None]
        outs.append((c[:, 1:] - lo) / cnt)
    mean = jnp.concatenate(outs, axis=-1)
    d = (mean - xf).astype(xa.dtype).reshape(B, S, POOL_GROUPS, POOL_GW)
    y = jnp.einsum('bsgc,gcd->bsgd', d, w_grp).reshape(B, S, POOL_W)
    return y * scale


def causal_depthwise_conv(x, w, b):
    S = x.shape[1]
    xp = jnp.pad(x, ((0, 0), (CONV_W - 1, 0), (0, 0)))
    y = b
    for k in range(CONV_W):
        y = y + xp[:, k:k + S] * w[k]
    return y


def rglru(xb, w_a, b_a, w_x, b_x, lam):
    B, S, _ = xb.shape
    xh = xb.reshape(B, S, LRU_HEADS, LRU_HD)
    r = jax.nn.sigmoid(jnp.einsum('bshc,hcd->bshd', xh, w_a).reshape(B, S, LRU_W) + b_a)
    i = jax.nn.sigmoid(jnp.einsum('bshc,hcd->bshd', xh, w_x).reshape(B, S, LRU_W) + b_x)
    log_a = -LRU_C * r.astype(jnp.float32) * jax.nn.softplus(-lam.astype(jnp.float32))
    a = jnp.exp(log_a)
    mult = jnp.sqrt(-jnp.expm1(2.0 * log_a))
    u = mult * (i * xb).astype(jnp.float32)

    def combine(l, rr):
        a1, b1 = l
        a2, b2 = rr
        return a1 * a2, a2 * b1 + b2

    _, h = lax.associative_scan(combine, (a, u), axis=1)
    return h.astype(xb.dtype)


def forgetting_attention(q, k, v, logf):
    B, S, _ = q.shape
    nb = S // Q_BLOCK
    q = q.reshape(B, S, FOX_HEADS, FOX_HD).transpose(0, 2, 1, 3)
    k = k.reshape(B, S, FOX_HEADS, FOX_HD).transpose(0, 2, 1, 3)
    v = v.reshape(B, S, FOX_HEADS, FOX_HD).transpose(0, 2, 1, 3)
    c = jnp.cumsum(logf.astype(jnp.float32), axis=1).transpose(0, 2, 1)
    qb = q.reshape(B, FOX_HEADS, nb, Q_BLOCK, FOX_HD).transpose(2, 0, 1, 3, 4)
    cb = c.reshape(B, FOX_HEADS, nb, Q_BLOCK).transpose(2, 0, 1, 3)
    starts = jnp.arange(nb, dtype=jnp.int32) * Q_BLOCK
    kpos = jnp.arange(S, dtype=jnp.int32)
    scale = FOX_HD ** -0.5

    def block(args):
        qi, ci, start = args
        s = jnp.einsum('bhqd,bhkd->bhqk', qi, k).astype(jnp.float32) * scale
        s = s + ci[..., None] - c[:, :, None, :]
        qpos = start + jnp.arange(Q_BLOCK, dtype=jnp.int32)
        s = jnp.where(kpos[None, :] <= qpos[:, None], s, -jnp.inf)
        p = jax.nn.softmax(s, axis=-1).astype(v.dtype)
        return jnp.einsum('bhqk,bhkd->bhqd', p, v)

    o = lax.map(block, (qb, cb, starts))
    return o.transpose(1, 0, 3, 2, 4).reshape(B, S, FOX_W)


def memory_cross_attention(h, m, w_q, w_kv, w_o):
    B, S, _ = h.shape
    q = (h @ w_q).reshape(B, S, X_HEADS, X_HD)
    k, v = jnp.split(m @ w_kv, 2, axis=-1)
    k = k.reshape(B, -1, X_HEADS, X_HD)
    v = v.reshape(B, -1, X_HEADS, X_HD)
    s = jnp.einsum('bshd,bmhd->bhsm', q, k).astype(jnp.float32) * (X_HD ** -0.5)
    p = jax.nn.softmax(s, axis=-1).astype(v.dtype)
    o = jnp.einsum('bhsm,bmhd->bshd', p, v).reshape(B, S, D_MODEL)
    return o @ w_o


def setup_inputs(seed: int = 0) -> dict:
    key = jax.random.key(seed)
    ks = iter(jax.random.split(key, 64))
    f32 = jnp.float32

    def dense(shape, fan_in):
        return jax.random.normal(next(ks), shape, f32) * (fan_in ** -0.5)

    def gain(shape):
        return 1.0 + 0.02 * jax.random.normal(next(ks), shape, f32)

    def bias(shape, s=0.01):
        return s * jax.random.normal(next(ks), shape, f32)

    L, D = DEPTH, D_MODEL
    a0 = jax.random.uniform(next(ks), (L, LRU_W), f32, 0.9, 0.999)
    s0 = a0 ** (1.0 / LRU_C)
    lru_lambda = jnp.log(s0) - jnp.log1p(-s0)
    return {
        "x": jax.random.normal(next(ks), (BATCH, SEQ, D), f32),
        "mem": jax.random.normal(next(ks), (BATCH, N_MEM, D), f32),
        "g_ffn1": gain((L, D)),
        "w_ffn1_in": dense((L, D, 2 * D_FF), D),
        "w_ffn1_out": dense((L, D_FF, D), D_FF),
        "g_mix": gain((L, D)),
        "w_in": dense((L, D, IN_W), D),
        "b_f": 2.0 + 0.5 * jax.random.normal(next(ks), (L, FOX_HEADS), f32),
        "b_gate": bias((L, N_BRANCH * D)),
        "w_pool": dense((L, POOL_GROUPS, POOL_GW, POOL_GW), POOL_GW),
        "pool_scale": gain((L, POOL_W)),
        "w_up_a": dense((L, POOL_W, D), POOL_W),
        "conv_w": dense((L, CONV_W, LRU_W), CONV_W),
        "conv_b": bias((L, LRU_W)),
        "w_rg_a": dense((L, LRU_HEADS, LRU_HD, LRU_HD), LRU_HD),
        "b_rg_a": bias((L, LRU_W)),
        "w_rg_x": dense((L, LRU_HEADS, LRU_HD, LRU_HD), LRU_HD),
        "b_rg_x": bias((L, LRU_W)),
        "lru_lambda": lru_lambda,
        "w_up_b": dense((L, LRU_W, D), LRU_W),
        "w_up_c": dense((L, FOX_W, D), FOX_W),
        "w_o": dense((L, D, D), D),
        "g_cross": gain((L, D)),
        "g_mem": gain((L, D)),
        "w_xq": dense((L, D, D), D),
        "w_xkv": dense((L, D, 2 * D), D),
        "w_xo": dense((L, D, D), D),
        "g_ffn2": gain((L, D)),
        "w_ffn2_in": dense((L, D, 2 * D_FF), D),
        "w_ffn2_out": dense((L, D_FF, D), D_FF),
        "g_final": gain((D,)),
    }


def reference(x, mem, g_ffn1, w_ffn1_in, w_ffn1_out, g_mix, w_in, b_f, b_gate, w_pool, pool_scale,
              w_up_a, conv_w, conv_b, w_rg_a, b_rg_a, w_rg_x, b_rg_x, lru_lambda, w_up_b, w_up_c, w_o,
              g_cross, g_mem, w_xq, w_xkv, w_xo, g_ffn2, w_ffn2_in, w_ffn2_out, g_final):
    B, S, D = x.shape
    offs = []
    acc = 0
    for n in IN_SIZES[:-1]:
        acc += n
        offs.append(acc)
    for l in range(DEPTH):
        x = x + 0.5 * swiglu(rmsnorm(x, g_ffn1[l]), w_ffn1_in[l], w_ffn1_out[l])
        u = rmsnorm(x, g_mix[l])
        xa, xb, gb, q, k, v, fl, gl = jnp.split(u @ w_in[l], offs, axis=-1)
        y_a = pool_mixer(xa, w_pool[l], pool_scale[l]) @ w_up_a[l]
        xb = causal_depthwise_conv(xb, conv_w[l], conv_b[l])
        h_b = rglru(xb, w_rg_a[l], b_rg_a[l], w_rg_x[l], b_rg_x[l], lru_lambda[l])
        y_b = (h_b * jax.nn.gelu(gb)) @ w_up_b[l]
        logf = jax.nn.log_sigmoid((fl + b_f[l]).astype(jnp.float32))
        y_c = forgetting_attention(q, k, v, logf) @ w_up_c[l]
        g = jax.nn.sigmoid(gl + b_gate[l]).reshape(B, S, N_BRANCH, D)
        merged = g[:, :, 0] * y_a + g[:, :, 1] * y_b + g[:, :, 2] * y_c
        x = x + merged @ w_o[l]
        x = x + memory_cross_attention(rmsnorm(x, g_cross[l]), rmsnorm(mem, g_mem[l]),
                                       w_xq[l], w_xkv[l], w_xo[l])
        x = x + 0.5 * swiglu(rmsnorm(x, g_ffn2[l]), w_ffn2_in[l], w_ffn2_out[l])
    return rmsnorm(x, g_final)
```

```python
import functools

import jax
import jax.numpy as jnp
from jax import lax
from jax.experimental import pallas as pl
from jax.experimental.pallas import tpu as pltpu

F32 = jnp.float32
BF16 = jnp.bfloat16

D_MODEL = 1024
N_MEM = 256
POOL_WINDOWS = (2, 4, 8, 16)
POOL_GROUPS = 4
POOL_W = 512
POOL_GW = 128
POOL_HALO = 16
LRU_W = 1024
LRU_HEADS = 8
LRU_HD = 128
CONV_W = 4
CONV_HALO = 8
LRU_C = 8.0
FOX_HEADS = 8
FOX_HD = 64
FOX_W = 512
X_HEADS = 4
X_HD = 256
D_FF = 2816
N_BRANCH = 3
EPS = 1e-6
SUBLANES = 8
LANES = 128
VMEM_LIMIT = 56 * 1024 * 1024
NEG_BIG = -1e30

OFF_XA = 0
OFF_XB = OFF_XA + POOL_W
OFF_GB = OFF_XB + LRU_W
OFF_Q = OFF_GB + LRU_W
OFF_K = OFF_Q + FOX_W
OFF_V = OFF_K + FOX_W
OFF_GL = OFF_V + FOX_W
IN_MAIN = OFF_GL + N_BRANCH * D_MODEL


def _params(n_grid_dims):
    return pltpu.CompilerParams(
        dimension_semantics=("arbitrary",) * n_grid_dims,
        vmem_limit_bytes=VMEM_LIMIT,
    )


def _const_spec(shape):
    nd = len(shape)
    return pl.BlockSpec(shape, lambda *_: (0,) * nd, pipeline_mode=pl.Buffered(1))


def _rms(x, g):
    ms = jnp.mean(x * x, axis=-1, keepdims=True)
    return x * lax.rsqrt(ms + EPS) * g


def _softplus(z):
    return jnp.maximum(z, 0.0) + jnp.log1p(jnp.exp(-jnp.abs(z)))


def _dot(a, b):
    return jnp.dot(a, b, preferred_element_type=F32)


def _dot_nt(a, b):
    return lax.dot_general(a, b, (((1,), (1,)), ((), ())), preferred_element_type=F32)


def _ffn_kernel(x_ref, g_ref, wa_ref, wb_ref, wo_ref, gf_ref, o_ref, *, n_chunks, final_norm):
    x = x_ref[...]
    h = _rms(x, g_ref[...]).astype(BF16)
    fc = D_FF // n_chunks
    acc = None
    for c in range(n_chunks):
        a = _dot(h, wa_ref[:, c * fc:(c + 1) * fc])
        b = _dot(h, wb_ref[:, c * fc:(c + 1) * fc])
        act = (a * jax.nn.sigmoid(a) * b).astype(BF16)
        part = _dot(act, wo_ref[c * fc:(c + 1) * fc, :])
        acc = part if acc is None else acc + part
    y = x + 0.5 * acc
    if final_norm:
        y = _rms(y, gf_ref[...])
    o_ref[...] = y


def _ffn(x, g, w_in_bf, w_out_bf, g_final, *, final_norm, tm=512, n_chunks=2):
    T = x.shape[0]
    return pl.pallas_call(
        functools.partial(_ffn_kernel, n_chunks=n_chunks, final_norm=final_norm),
        grid=(T // tm,),
        in_specs=[
            pl.BlockSpec((tm, D_MODEL), lambda i: (i, 0)),
            _const_spec((1, D_MODEL)),
            pl.BlockSpec((D_MODEL, D_FF), lambda i: (0, 0), pipeline_mode=pl.Buffered(1)),
            pl.BlockSpec((D_MODEL, D_FF), lambda i: (0, 1), pipeline_mode=pl.Buffered(1)),
            _const_spec((D_FF, D_MODEL)),
            _const_spec((1, D_MODEL)),
        ],
        out_specs=pl.BlockSpec((tm, D_MODEL), lambda i: (i, 0)),
        out_shape=jax.ShapeDtypeStruct((T, D_MODEL), F32),
        compiler_params=_params(1),
        name="ffn",
    )(x, g, w_in_bf, w_in_bf, w_out_bf, g_final)


def _inproj_kernel(x_ref, g_ref, w_ref, wfl_ref,
                   xa_ref, xb_ref, gb_ref, q_ref, k_ref, v_ref, gl_ref, fl_ref):
    h = _rms(x_ref[...], g_ref[...]).astype(BF16)

    def seg(lo, width):
        return _dot(h, w_ref[:, lo:lo + width])

    xa_ref[...] = seg(OFF_XA, POOL_W)
    xb_ref[...] = seg(OFF_XB, LRU_W)
    gb_ref[...] = seg(OFF_GB, LRU_W)
    q_ref[...] = seg(OFF_Q, FOX_W).astype(BF16)
    k_ref[...] = seg(OFF_K, FOX_W).astype(BF16)
    v_ref[...] = seg(OFF_V, FOX_W).astype(BF16)
    gl_ref[...] = seg(OFF_GL, N_BRANCH * D_MODEL)
    fl_ref[...] = _dot_nt(wfl_ref[...], h)


def _inproj(x, g, w_main_bf, w_fl_t_bf, *, tm=256):
    T = x.shape[0]

    def rows(width):
        return pl.BlockSpec((tm, width), lambda i: (i, 0))

    out_shape = (
        jax.ShapeDtypeStruct((T, POOL_W), F32),
        jax.ShapeDtypeStruct((T, LRU_W), F32),
        jax.ShapeDtypeStruct((T, LRU_W), F32),
        jax.ShapeDtypeStruct((T, FOX_W), BF16),
        jax.ShapeDtypeStruct((T, FOX_W), BF16),
        jax.ShapeDtypeStruct((T, FOX_W), BF16),
        jax.ShapeDtypeStruct((T, N_BRANCH * D_MODEL), F32),
        jax.ShapeDtypeStruct((FOX_HEADS, T), F32),
    )
    out_specs = (
        rows(POOL_W), rows(LRU_W), rows(LRU_W), rows(FOX_W), rows(FOX_W), rows(FOX_W),
        rows(N_BRANCH * D_MODEL),
        pl.BlockSpec((FOX_HEADS, tm), lambda i: (0, i)),
    )
    return pl.pallas_call(
        _inproj_kernel,
        grid=(T // tm,),
        in_specs=[
            rows(D_MODEL),
            _const_spec((1, D_MODEL)),
            _const_spec((D_MODEL, IN_MAIN)),
            _const_spec((FOX_HEADS, D_MODEL)),
        ],
        out_specs=out_specs,
        out_shape=out_shape,
        compiler_params=_params(1),
        name="inproj",
    )(x, g, w_main_bf, w_fl_t_bf)


def _pool_kernel(xa_ref, wp_ref, sc_ref, o_ref, ext_ref, *, ts):
    s = pl.program_id(1)

    @pl.when(s == 0)
    def _():
        ext_ref[0:POOL_HALO, :] = jnp.zeros((POOL_HALO, POOL_W), F32)

    x = xa_ref[...]
    ext_ref[POOL_HALO:POOL_HALO + ts, :] = x
    e = ext_ref[...]
    ext_ref[0:POOL_HALO, :] = e[ts:ts + POOL_HALO, :]

    sums = []
    cur = e
    shift = 1
    for g in range(POOL_GROUPS):
        cur = cur + pltpu.roll(cur, shift, 0)
        sums.append(cur[POOL_HALO:, 0:POOL_GW])
        if g + 1 < POOL_GROUPS:
            cur = cur[:, POOL_GW:]
        shift *= 2

    pos = (s * ts + 1 + lax.broadcasted_iota(jnp.int32, (ts, 1), 0)).astype(F32)
    for g, w in enumerate(POOL_WINDOWS):
        inv_cnt = 1.0 / jnp.minimum(pos, float(w))
        mean = sums[g] * inv_cnt
        d = (mean - x[:, g * POOL_GW:(g + 1) * POOL_GW]).astype(BF16)
        y = _dot(d, wp_ref[g]) * sc_ref[:, g * POOL_GW:(g + 1) * POOL_GW]
        o_ref[:, g * POOL_GW:(g + 1) * POOL_GW] = y.astype(BF16)


def _pool(xa, w_pool_bf, scale, *, ts=512):
    B, S, _ = xa.shape
    return pl.pallas_call(
        functools.partial(_pool_kernel, ts=ts),
        grid=(B, S // ts),
        in_specs=[
            pl.BlockSpec((None, ts, POOL_W), lambda b, s: (b, s, 0)),
            _const_spec((POOL_GROUPS, POOL_GW, POOL_GW)),
            _const_spec((1, POOL_W)),
        ],
        out_specs=pl.BlockSpec((None, ts, POOL_W), lambda b, s: (b, s, 0)),
        out_shape=jax.ShapeDtypeStruct((B, S, POOL_W), BF16),
        scratch_shapes=[pltpu.VMEM((ts + POOL_HALO, POOL_W), F32)],
        compiler_params=_params(2),
        name="pool",
    )(xa, w_pool_bf, scale)


def _rglru_kernel(xb_ref, gb_ref, cw_ref, cb_ref, wg_ref, ba_ref, bx_ref, lam_ref, o_ref,
                  ext_ref, a_ref, u_ref, hc_ref, *, ts, lane_chunk):
    s = pl.program_id(1)

    @pl.when(s == 0)
    def _():
        ext_ref[0:CONV_HALO, :] = jnp.zeros((CONV_HALO, LRU_W), F32)
        hc_ref[...] = jnp.zeros((SUBLANES, LRU_W), F32)

    ext_ref[CONV_HALO:CONV_HALO + ts, :] = xb_ref[...]
    y = cb_ref[...]
    for k in range(CONV_W):
        lo = CONV_HALO - (CONV_W - 1) + k
        y = y + ext_ref[lo:lo + ts, :] * cw_ref[k:k + 1, :]
    ext_ref[0:CONV_HALO, :] = ext_ref[ts:ts + CONV_HALO, :]

    yb = y.astype(BF16)
    for h in range(LRU_HEADS):
        cols = slice(h * LRU_HD, (h + 1) * LRU_HD)
        gates = _dot(yb[:, cols], wg_ref[h])
        r = jax.nn.sigmoid(gates[:, :LRU_HD] + ba_ref[:, cols])
        i = jax.nn.sigmoid(gates[:, LRU_HD:] + bx_ref[:, cols])
        log_a = -LRU_C * r * _softplus(-lam_ref[:, cols])
        a = jnp.exp(log_a)
        one_minus_a2 = -jnp.tanh(log_a) * (a * a + 1.0)
        a_ref[:, cols] = a
        u_ref[:, cols] = jnp.sqrt(one_minus_a2) * (i * y[:, cols])

    row = lax.broadcasted_iota(jnp.int32, (SUBLANES, lane_chunk), 0)
    for c in range(LRU_W // lane_chunk):
        cols = slice(c * lane_chunk, (c + 1) * lane_chunk)

        def group(gi, h_prev, cols=cols):
            off = pl.multiple_of(gi * SUBLANES, SUBLANES)
            a = a_ref[pl.ds(off, SUBLANES), cols]
            u = u_ref[pl.ds(off, SUBLANES), cols]
            for d in (1, 2, 4):
                a_sh = jnp.where(row >= d, pltpu.roll(a, d, 0), 1.0)
                u_sh = jnp.where(row >= d, pltpu.roll(u, d, 0), 0.0)
                u = a * u_sh + u
                a = a * a_sh
            hh = a * h_prev + u
            u_ref[pl.ds(off, SUBLANES), cols] = hh
            return jnp.broadcast_to(hh[SUBLANES - 1:SUBLANES, :], (SUBLANES, lane_chunk))

        hc_ref[:, cols] = lax.fori_loop(0, ts // SUBLANES, group, hc_ref[:, cols], unroll=2)

    o_ref[...] = (u_ref[...] * jax.nn.gelu(gb_ref[...])).astype(BF16)


def _rglru(xb, gb, conv_w, conv_b, wg_bf, b_a, b_x, lam, *, ts=256, lane_chunk=512):
    B, S, _ = xb.shape
    tile = pl.BlockSpec((None, ts, LRU_W), lambda b, s: (b, s, 0))
    return pl.pallas_call(
        functools.partial(_rglru_kernel, ts=ts, lane_chunk=lane_chunk),
        grid=(B, S // ts),
        in_specs=[
            tile, tile,
            _const_spec((CONV_W, LRU_W)),
            _const_spec((1, LRU_W)),
            _const_spec((LRU_HEADS, LRU_HD, 2 * LRU_HD)),
            _const_spec((1, LRU_W)),
            _const_spec((1, LRU_W)),
            _const_spec((1, LRU_W)),
        ],
        out_specs=tile,
        out_shape=jax.ShapeDtypeStruct((B, S, LRU_W), BF16),
        scratch_shapes=[
            pltpu.VMEM((ts + CONV_HALO, LRU_W), F32),
            pltpu.VMEM((ts, LRU_W), F32),
            pltpu.VMEM((ts, LRU_W), F32),
            pltpu.VMEM((SUBLANES, LRU_W), F32),
        ],
        compiler_params=_params(2),
        name="rglru",
    )(xb, gb, conv_w, conv_b, wg_bf, b_a, b_x, lam)


def _logf_cumsum_kernel(fl_ref, bf_ref, o_ref, *, seq):
    z = fl_ref[...] + bf_ref[...]
    x = jnp.minimum(z, 0.0) - jnp.log1p(jnp.exp(-jnp.abs(z)))
    lane = lax.broadcasted_iota(jnp.int32, x.shape, 1)
    d = 1
    while d < seq:
        x = x + jnp.where(lane >= d, pltpu.roll(x, d, 1), 0.0)
        d *= 2
    o_ref[...] = -x


def _logf_cumsum(fl_t, b_f, *, seq):
    H, T = fl_t.shape
    return pl.pallas_call(
        functools.partial(_logf_cumsum_kernel, seq=seq),
        grid=(T // seq,),
        in_specs=[
            pl.BlockSpec((H, seq), lambda b: (0, b)),
            _const_spec((H, 1)),
        ],
        out_specs=pl.BlockSpec((H, seq), lambda b: (0, b)),
        out_shape=jax.ShapeDtypeStruct((H, T), F32),
        compiler_params=_params(1),
        name="logf_cumsum",
    )(fl_t, b_f)


def _fox_kernel(q_ref, k_ref, v_ref, c_ref, o_ref, *, tq):
    qi = pl.program_id(2)
    q = q_ref[...]
    lane = lax.broadcasted_iota(jnp.int32, (tq, 2 * FOX_HD), 1)
    rowi = lax.broadcasted_iota(jnp.int32, (tq, tq), 0)
    coli = lax.broadcasted_iota(jnp.int32, (tq, tq), 1)
    scale = FOX_HD ** -0.5
    outs = []
    for hh in range(2):
        in_head = (lane < FOX_HD) if hh == 0 else (lane >= FOX_HD)
        qh = jnp.where(in_head, q, jnp.zeros_like(q)) * jnp.asarray(scale, BF16)

        def step(j, carry, masked, hh=hh, qh=qh):
            m, l, acc = carry
            off = pl.multiple_of(j * tq, tq)
            kj = k_ref[pl.ds(off, tq), :]
            vj = v_ref[pl.ds(off, tq), :]
            s = _dot_nt(qh, kj) + c_ref[hh:hh + 1, pl.ds(off, tq)]
            if masked:
                s = jnp.where(coli <= rowi, s, NEG_BIG)
            m_new = jnp.maximum(m, jnp.max(s, axis=-1, keepdims=True))
            alpha = jnp.exp(m - m_new)
            p = jnp.exp(s - m_new)
            l_new = alpha * l + jnp.sum(p, axis=-1, keepdims=True)
            acc_new = alpha * acc + _dot(p.astype(BF16), vj)
            return m_new, l_new, acc_new

        init = (jnp.full((tq, 1), -jnp.inf, F32), jnp.zeros((tq, 1), F32),
                jnp.zeros((tq, 2 * FOX_HD), F32))
        carry = lax.fori_loop(0, qi, functools.partial(step, masked=False), init)
        m, l, acc = step(qi, carry, True)
        outs.append(acc / l)
    o_ref[...] = jnp.where(lane < FOX_HD, outs[0], outs[1]).astype(BF16)


def _fox(q, k, v, cneg, *, tq=256):
    B, S, _ = q.shape
    n_pairs = FOX_HEADS // 2
    return pl.pallas_call(
        functools.partial(_fox_kernel, tq=tq),
        grid=(B, n_pairs, S // tq),
        in_specs=[
            pl.BlockSpec((None, tq, 2 * FOX_HD), lambda b, p, i: (b, i, p)),
            pl.BlockSpec((None, S, 2 * FOX_HD), lambda b, p, i: (b, 0, p)),
            pl.BlockSpec((None, S, 2 * FOX_HD), lambda b, p, i: (b, 0, p)),
            pl.BlockSpec((None, None, 2, S), lambda b, p, i: (b, p, 0, 0)),
        ],
        out_specs=pl.BlockSpec((None, tq, 2 * FOX_HD), lambda b, p, i: (b, i, p)),
        out_shape=jax.ShapeDtypeStruct((B, S, FOX_W), BF16),
        compiler_params=_params(3),
        name="fox_attention",
    )(q, k, v, cneg)


def _merge_kernel(x_ref, a_ref, b_ref, c_ref, gl_ref, bg_ref,
                  ua_ref, ub_ref, uc_ref, wo_ref, o_ref):
    merged = None
    for n, (br_ref, up_ref) in enumerate(((a_ref, ua_ref), (b_ref, ub_ref), (c_ref, uc_ref))):
        cols = slice(n * D_MODEL, (n + 1) * D_MODEL)
        gate = jax.nn.sigmoid(gl_ref[:, cols] + bg_ref[:, cols])
        term = gate * _dot(br_ref[...], up_ref[...])
        merged = term if merged is None else merged + term
    o_ref[...] = x_ref[...] + _dot(merged.astype(BF16), wo_ref[...])


def _merge(x, ya, yb, yc, gl, b_gate, ua, ub, uc, wo, *, tm=512):
    T = x.shape[0]

    def rows(width):
        return pl.BlockSpec((tm, width), lambda i: (i, 0))

    return pl.pallas_call(
        _merge_kernel,
        grid=(T // tm,),
        in_specs=[
            rows(D_MODEL), rows(POOL_W), rows(LRU_W), rows(FOX_W), rows(N_BRANCH * D_MODEL),
            _const_spec((1, N_BRANCH * D_MODEL)),
            _const_spec((POOL_W, D_MODEL)),
            _const_spec((LRU_W, D_MODEL)),
            _const_spec((FOX_W, D_MODEL)),
            _const_spec((D_MODEL, D_MODEL)),
        ],
        out_specs=rows(D_MODEL),
        out_shape=jax.ShapeDtypeStruct((T, D_MODEL), F32),
        compiler_params=_params(1),
        name="merge",
    )(x, ya, yb, yc, gl, b_gate, ua, ub, uc, wo)


def _memkv_kernel(m_ref, g_ref, w_ref, k_ref, v_ref):
    h = _rms(m_ref[...], g_ref[...]).astype(BF16)
    k_ref[...] = _dot(h, w_ref[:, :D_MODEL]).astype(BF16)
    v_ref[...] = _dot(h, w_ref[:, D_MODEL:]).astype(BF16)


def _memkv(mem, g, w_kv_bf):
    B = mem.shape[0]
    tile = pl.BlockSpec((None, N_MEM, D_MODEL), lambda b: (b, 0, 0))
    return pl.pallas_call(
        _memkv_kernel,
        grid=(B,),
        in_specs=[tile, _const_spec((1, D_MODEL)), _const_spec((D_MODEL, 2 * D_MODEL))],
        out_specs=(tile, tile),
        out_shape=(jax.ShapeDtypeStruct((B, N_MEM, D_MODEL), BF16),) * 2,
        compiler_params=_params(1),
        name="mem_kv",
    )(mem, g, w_kv_bf)


def _cross_kernel(x_ref, g_ref, wq_ref, k_ref, v_ref, wo_ref, o_ref):
    x = x_ref[...]
    h = _rms(x, g_ref[...]).astype(BF16)
    q = (_dot(h, wq_ref[...]) * (X_HD ** -0.5)).astype(BF16)
    heads = []
    for hd in range(X_HEADS):
        cols = slice(hd * X_HD, (hd + 1) * X_HD)
        s = _dot_nt(q[:, cols], k_ref[:, cols])
        e = jnp.exp(s - jnp.max(s, axis=-1, keepdims=True))
        p = e / jnp.sum(e, axis=-1, keepdims=True)
        heads.append(_dot(p.astype(BF16), v_ref[:, cols]).astype(BF16))
    o = jnp.concatenate(heads, axis=-1)
    o_ref[...] = x + _dot(o, wo_ref[...])


def _cross(x, g, wq, k, v, wo, *, tm=512):
    B, S, _ = x.shape
    tile = pl.BlockSpec((None, tm, D_MODEL), lambda b, s: (b, s, 0))
    kv = pl.BlockSpec((None, N_MEM, D_MODEL), lambda b, s: (b, 0, 0))
    return pl.pallas_call(
        _cross_kernel,
        grid=(B, S // tm),
        in_specs=[tile, _const_spec((1, D_MODEL)), _const_spec((D_MODEL, D_MODEL)), kv, kv,
                  _const_spec((D_MODEL, D_MODEL))],
        out_specs=tile,
        out_shape=jax.ShapeDtypeStruct((B, S, D_MODEL), F32),
        compiler_params=_params(2),
        name="cross_attention",
    )(x, g, wq, k, v, wo)


def kernel(x, mem, g_ffn1, w_ffn1_in, w_ffn1_out, g_mix, w_in, b_f, b_gate, w_pool, pool_scale,
           w_up_a, conv_w, conv_b, w_rg_a, b_rg_a, w_rg_x, b_rg_x, lru_lambda, w_up_b, w_up_c, w_o,
           g_cross, g_mem, w_xq, w_xkv, w_xo, g_ffn2, w_ffn2_in, w_ffn2_out, g_final):
    B, S, D = x.shape
    T = B * S
    depth = g_ffn1.shape[0]
    off_fl = OFF_GL
    row = lambda vec: vec.reshape(1, -1)
    g_fin = row(g_final)

    xt = x.reshape(T, D)
    for l in range(depth):
        xt = _ffn(xt, row(g_ffn1[l]), w_ffn1_in[l].astype(BF16), w_ffn1_out[l].astype(BF16),
                  g_fin, final_norm=False)

        w_l = w_in[l]
        w_main = jnp.concatenate([w_l[:, :off_fl], w_l[:, off_fl + FOX_HEADS:]], axis=1).astype(BF16)
        w_fl_t = w_l[:, off_fl:off_fl + FOX_HEADS].T.astype(BF16)
        xa, xb, gb, q, k, v, gl, fl_t = _inproj(xt, row(g_mix[l]), w_main, w_fl_t)

        ya = _pool(xa.reshape(B, S, POOL_W), w_pool[l].astype(BF16), row(pool_scale[l]))
        wg = jnp.concatenate([w_rg_a[l], w_rg_x[l]], axis=-1).astype(BF16)
        yb = _rglru(xb.reshape(B, S, LRU_W), gb.reshape(B, S, LRU_W), conv_w[l], row(conv_b[l]),
                    wg, row(b_rg_a[l]), row(b_rg_x[l]), row(lru_lambda[l]))
        cneg = _logf_cumsum(fl_t, b_f[l].reshape(FOX_HEADS, 1), seq=S)
        cneg = cneg.reshape(FOX_HEADS // 2, 2, B, S).transpose(2, 0, 1, 3)
        yc = _fox(q.reshape(B, S, FOX_W), k.reshape(B, S, FOX_W), v.reshape(B, S, FOX_W), cneg)

        xt = _merge(xt, ya.reshape(T, POOL_W), yb.reshape(T, LRU_W), yc.reshape(T, FOX_W), gl,
                    row(b_gate[l]), w_up_a[l].astype(BF16), w_up_b[l].astype(BF16),
                    w_up_c[l].astype(BF16), w_o[l].astype(BF16))

        mk, mv = _memkv(mem, row(g_mem[l]), w_xkv[l].astype(BF16))
        xt = _cross(xt.reshape(B, S, D), row(g_cross[l]), w_xq[l].astype(BF16), mk, mv,
                    w_xo[l].astype(BF16)).reshape(T, D)

        xt = _ffn(xt, row(g_ffn2[l]), w_ffn2_in[l].astype(BF16), w_ffn2_out[l].astype(BF16),
                  g_fin, final_norm=(l == depth - 1))
    return xt.reshape(B, S, D)
```

```python
import functools

import jax
import jax.numpy as jnp
from jax import lax
from jax.experimental import pallas as pl
from jax.experimental.pallas import tpu as pltpu

F32 = jnp.float32
BF16 = jnp.bfloat16

D_MODEL = 1024
N_MEM = 256
POOL_WINDOWS = (2, 4, 8, 16)
POOL_GROUPS = 4
POOL_W = 512
POOL_GW = 128
POOL_HALO = 16
LRU_W = 1024
LRU_HEADS = 8
LRU_HD = 128
CONV_W = 4
CONV_HALO = 8
LRU_C = 8.0
FOX_HEADS = 8
FOX_HD = 64
FOX_W = 512
X_HEADS = 4
X_HD = 256
D_FF = 2816
N_BRANCH = 3
EPS = 1e-6
SUBLANES = 8
LANES = 128
VMEM_LIMIT = 56 * 1024 * 1024
NEG_BIG = -1e30

OFF_XA = 0
OFF_XB = OFF_XA + POOL_W
OFF_GB = OFF_XB + LRU_W
OFF_Q = OFF_GB + LRU_W
OFF_K = OFF_Q + FOX_W
OFF_V = OFF_K + FOX_W
OFF_GL = OFF_V + FOX_W
IN_MAIN = OFF_GL + N_BRANCH * D_MODEL


def _params(n_grid_dims):
    return pltpu.CompilerParams(
        dimension_semantics=("arbitrary",) * n_grid_dims,
        vmem_limit_bytes=VMEM_LIMIT,
    )


def _const_spec(shape):
    nd = len(shape)
    return pl.BlockSpec(shape, lambda *_: (0,) * nd, pipeline_mode=pl.Buffered(1))


def _rms(x, g):
    ms = jnp.mean(x * x, axis=-1, keepdims=True)
    return x * lax.rsqrt(ms + EPS) * g


def _softplus(z):
    return jnp.maximum(z, 0.0) + jnp.log1p(jnp.exp(-jnp.abs(z)))


def _dot(a, b):
    return jnp.dot(a, b, preferred_element_type=F32)


def _dot_nt(a, b):
    return lax.dot_general(a, b, (((1,), (1,)), ((), ())), preferred_element_type=F32)


def _ffn_kernel(x_ref, g_ref, wa_ref, wb_ref, wo_ref, gf_ref, o_ref, *, n_chunks, final_norm):
    x = x_ref[...]
    h = _rms(x, g_ref[...]).astype(BF16)
    fc = D_FF // n_chunks
    acc = None
    for c in range(n_chunks):
        a = _dot(h, wa_ref[:, c * fc:(c + 1) * fc])
        b = _dot(h, wb_ref[:, c * fc:(c + 1) * fc])
        act = (a * jax.nn.sigmoid(a) * b).astype(BF16)
        part = _dot(act, wo_ref[c * fc:(c + 1) * fc, :])
        acc = part if acc is None else acc + part
    y = x + 0.5 * acc
    if final_norm:
        y = _rms(y, gf_ref[...])
    o_ref[...] = y


def _ffn(x, g, w_in_bf, w_out_bf, g_final, *, final_norm, tm=512, n_chunks=2):
    T = x.shape[0]
    return pl.pallas_call(
        functools.partial(_ffn_kernel, n_chunks=n_chunks, final_norm=final_norm),
        grid=(T // tm,),
        in_specs=[
            pl.BlockSpec((tm, D_MODEL), lambda i: (i, 0)),
            _const_spec((1, D_MODEL)),
            pl.BlockSpec((D_MODEL, D_FF), lambda i: (0, 0), pipeline_mode=pl.Buffered(1)),
            pl.BlockSpec((D_MODEL, D_FF), lambda i: (0, 1), pipeline_mode=pl.Buffered(1)),
            _const_spec((D_FF, D_MODEL)),
            _const_spec((1, D_MODEL)),
        ],
        out_specs=pl.BlockSpec((tm, D_MODEL), lambda i: (i, 0)),
        out_shape=jax.ShapeDtypeStruct((T, D_MODEL), F32),
        compiler_params=_params(1),
        name="ffn",
    )(x, g, w_in_bf, w_in_bf, w_out_bf, g_final)


def _inproj_kernel(x_ref, g_ref, w_ref, wfl_ref,
                   xa_ref, xb_ref, gb_ref, q_ref, k_ref, v_ref, gl_ref, fl_ref):
    h = _rms(x_ref[...], g_ref[...]).astype(BF16)

    def seg(lo, width):
        return _dot(h, w_ref[:, lo:lo + width])

    xa_ref[...] = seg(OFF_XA, POOL_W)
    xb_ref[...] = seg(OFF_XB, LRU_W)
    gb_ref[...] = seg(OFF_GB, LRU_W)
    q_ref[...] = seg(OFF_Q, FOX_W).astype(BF16)
    k_ref[...] = seg(OFF_K, FOX_W).astype(BF16)
    v_ref[...] = seg(OFF_V, FOX_W).astype(BF16)
    gl_ref[...] = seg(OFF_GL, N_BRANCH * D_MODEL)
    fl_ref[...] = _dot_nt(wfl_ref[...], h)


def _inproj(x, g, w_main_bf, w_fl_t_bf, *, tm=256):
    T = x.shape[0]

    def rows(width):
        return pl.BlockSpec((tm, width), lambda i: (i, 0))

    out_shape = (
        jax.ShapeDtypeStruct((T, POOL_W), F32),
        jax.ShapeDtypeStruct((T, LRU_W), F32),
        jax.ShapeDtypeStruct((T, LRU_W), F32),
        jax.ShapeDtypeStruct((T, FOX_W), BF16),
        jax.ShapeDtypeStruct((T, FOX_W), BF16),
        jax.ShapeDtypeStruct((T, FOX_W), BF16),
        jax.ShapeDtypeStruct((T, N_BRANCH * D_MODEL), F32),
        jax.ShapeDtypeStruct((FOX_HEADS, T), F32),
    )
    out_specs = (
        rows(POOL_W), rows(LRU_W), rows(LRU_W), rows(FOX_W), rows(FOX_W), rows(FOX_W),
        rows(N_BRANCH * D_MODEL),
        pl.BlockSpec((FOX_HEADS, tm), lambda i: (0, i)),
    )
    return pl.pallas_call(
        _inproj_kernel,
        grid=(T // tm,),
        in_specs=[
            rows(D_MODEL),
            _const_spec((1, D_MODEL)),
            _const_spec((D_MODEL, IN_MAIN)),
            _const_spec((FOX_HEADS, D_MODEL)),
        ],
        out_specs=out_specs,
        out_shape=out_shape,
        compiler_params=_params(1),
        name="inproj",
    )(x, g, w_main_bf, w_fl_t_bf)


def _pool_kernel(xa_ref, wp_ref, sc_ref, o_ref, ext_ref, *, ts):
    s = pl.program_id(1)

    @pl.when(s == 0)
    def _():
        ext_ref[0:POOL_HALO, :] = jnp.zeros((POOL_HALO, POOL_W), F32)

    x = xa_ref[...]
    ext_ref[POOL_HALO:POOL_HALO + ts, :] = x
    e = ext_ref[...]
    ext_ref[0:POOL_HALO, :] = e[ts:ts + POOL_HALO, :]

    sums = []
    cur = e
    shift = 1
    for g in range(POOL_GROUPS):
        cur = cur + pltpu.roll(cur, shift, 0)
        sums.append(cur[POOL_HALO:, 0:POOL_GW])
        if g + 1 < POOL_GROUPS:
            cur = cur[:, POOL_GW:]
        shift *= 2

    pos = (s * ts + 1 + lax.broadcasted_iota(jnp.int32, (ts, 1), 0)).astype(F32)
    for g, w in enumerate(POOL_WINDOWS):
        inv_cnt = 1.0 / jnp.minimum(pos, float(w))
        mean = sums[g] * inv_cnt
        d = (mean - x[:, g * POOL_GW:(g + 1) * POOL_GW]).astype(BF16)
        y = _dot(d, wp_ref[g]) * sc_ref[:, g * POOL_GW:(g + 1) * POOL_GW]
        o_ref[:, g * POOL_GW:(g + 1) * POOL_GW] = y.astype(BF16)


def _pool(xa, w_pool_bf, scale, *, ts=512):
    B, S, _ = xa.shape
    return pl.pallas_call(
        functools.partial(_pool_kernel, ts=ts),
        grid=(B, S // ts),
        in_specs=[
            pl.BlockSpec((None, ts, POOL_W), lambda b, s: (b, s, 0)),
            _const_spec((POOL_GROUPS, POOL_GW, POOL_GW)),
            _const_spec((1, POOL_W)),
        ],
        out_specs=pl.BlockSpec((None, ts, POOL_W), lambda b, s: (b, s, 0)),
        out_shape=jax.ShapeDtypeStruct((B, S, POOL_W), BF16),
        scratch_shapes=[pltpu.VMEM((ts + POOL_HALO, POOL_W), F32)],
        compiler_params=_params(2),
        name="pool",
    )(xa, w_pool_bf, scale)


def _rglru_kernel(xb_ref, gb_ref, cw_ref, cb_ref, wg_ref, ba_ref, bx_ref, lam_ref, o_ref,
                  ext_ref, a_ref, u_ref, hc_ref, *, ts, lane_chunk):
    s = pl.program_id(1)

    @pl.when(s == 0)
    def _():
        ext_ref[0:CONV_HALO, :] = jnp.zeros((CONV_HALO, LRU_W), F32)
        hc_ref[...] = jnp.zeros((SUBLANES, LRU_W), F32)

    ext_ref[CONV_HALO:CONV_HALO + ts, :] = xb_ref[...]
    y = cb_ref[...]
    for k in range(CONV_W):
        lo = CONV_HALO - (CONV_W - 1) + k
        y = y + ext_ref[lo:lo + ts, :] * cw_ref[k:k + 1, :]
    ext_ref[0:CONV_HALO, :] = ext_ref[ts:ts + CONV_HALO, :]

    yb = y.astype(BF16)
    for h in range(LRU_HEADS):
        cols = slice(h * LRU_HD, (h + 1) * LRU_HD)
        gates = _dot(yb[:, cols], wg_ref[h])
        r = jax.nn.sigmoid(gates[:, :LRU_HD] + ba_ref[:, cols])
        i = jax.nn.sigmoid(gates[:, LRU_HD:] + bx_ref[:, cols])
        log_a = -LRU_C * r * _softplus(-lam_ref[:, cols])
        a = jnp.exp(log_a)
        one_minus_a2 = -jnp.tanh(log_a) * (a * a + 1.0)
        a_ref[:, cols] = a
        u_ref[:, cols] = jnp.sqrt(one_minus_a2) * (i * y[:, cols])

    row = lax.broadcasted_iota(jnp.int32, (SUBLANES, lane_chunk), 0)
    for c in range(LRU_W // lane_chunk):
        cols = slice(c * lane_chunk, (c + 1) * lane_chunk)

        def group(gi, h_prev, cols=cols):
            off = pl.multiple_of(gi * SUBLANES, SUBLANES)
            a = a_ref[pl.ds(off, SUBLANES), cols]
            u = u_ref[pl.ds(off, SUBLANES), cols]
            for d in (1, 2, 4):
                a_sh = jnp.where(row >= d, pltpu.roll(a, d, 0), 1.0)
                u_sh = jnp.where(row >= d, pltpu.roll(u, d, 0), 0.0)
                u = a * u_sh + u
                a = a * a_sh
            hh = a * h_prev + u
            u_ref[pl.ds(off, SUBLANES), cols] = hh
            return jnp.broadcast_to(hh[SUBLANES - 1:SUBLANES, :], (SUBLANES, lane_chunk))

        hc_ref[:, cols] = lax.fori_loop(0, ts // SUBLANES, group, hc_ref[:, cols], unroll=2)

    o_ref[...] = (u_ref[...] * jax.nn.gelu(gb_ref[...])).astype(BF16)


def _rglru(xb, gb, conv_w, conv_b, wg_bf, b_a, b_x, lam, *, ts=256, lane_chunk=512):
    B, S, _ = xb.shape
    tile = pl.BlockSpec((None, ts, LRU_W), lambda b, s: (b, s, 0))
    return pl.pallas_call(
        functools.partial(_rglru_kernel, ts=ts, lane_chunk=lane_chunk),
        grid=(B, S // ts),
        in_specs=[
            tile, tile,
            _const_spec((CONV_W, LRU_W)),
            _const_spec((1, LRU_W)),
            _const_spec((LRU_HEADS, LRU_HD, 2 * LRU_HD)),
            _const_spec((1, LRU_W)),
            _const_spec((1, LRU_W)),
            _const_spec((1, LRU_W)),
        ],
        out_specs=tile,
        out_shape=jax.ShapeDtypeStruct((B, S, LRU_W), BF16),
        scratch_shapes=[
            pltpu.VMEM((ts + CONV_HALO, LRU_W), F32),
            pltpu.VMEM((ts, LRU_W), F32),
            pltpu.VMEM((ts, LRU_W), F32),
            pltpu.VMEM((SUBLANES, LRU_W), F32),
        ],
        compiler_params=_params(2),
        name="rglru",
    )(xb, gb, conv_w, conv_b, wg_bf, b_a, b_x, lam)


def _logf_cumsum_kernel(fl_ref, bf_ref, o_ref, *, seq):
    z = fl_ref[...] + bf_ref[...]
    x = jnp.minimum(z, 0.0) - jnp.log1p(jnp.exp(-jnp.abs(z)))
    lane = lax.broadcasted_iota(jnp.int32, x.shape, 1)
    d = 1
    while d < seq:
        x = x + jnp.where(lane >= d, pltpu.roll(x, d, 1), 0.0)
        d *= 2
    o_ref[...] = -x


def _logf_cumsum(fl_t, b_f, *, seq):
    H, T = fl_t.shape
    return pl.pallas_call(
        functools.partial(_logf_cumsum_kernel, seq=seq),
        grid=(T // seq,),
        in_specs=[
            pl.BlockSpec((H, seq), lambda b: (0, b)),
            _const_spec((H, 1)),
        ],
        out_specs=pl.BlockSpec((H, seq), lambda b: (0, b)),
        out_shape=jax.ShapeDtypeStruct((H, T), F32),
        compiler_params=_params(1),
        name="logf_cumsum",
    )(fl_t, b_f)


def _fox_kernel(q_ref, k_ref, v_ref, c_ref, o_ref, *, tq):
    qi = pl.program_id(2)
    q = q_ref[...]
    lane = lax.broadcasted_iota(jnp.int32, (tq, 2 * FOX_HD), 1)
    scale = jnp.asarray(FOX_HD ** -0.5, BF16)
    zero = jnp.zeros_like(q)
    qh = (jnp.where(lane < FOX_HD, q, zero) * scale, jnp.where(lane >= FOX_HD, q, zero) * scale)

    def step(j, carry, masked):
        off = pl.multiple_of(j * tq, tq)
        kj = k_ref[pl.ds(off, tq), :]
        vj = v_ref[pl.ds(off, tq), :]
        new = []
        for hh in range(2):
            m, l, acc = carry[hh]
            s = _dot_nt(qh[hh], kj) + c_ref[hh:hh + 1, pl.ds(off, tq)]
            if masked:
                rowi = lax.broadcasted_iota(jnp.int32, (tq, tq), 0)
                coli = lax.broadcasted_iota(jnp.int32, (tq, tq), 1)
                s = jnp.where(coli <= rowi, s, NEG_BIG)
            m_new = jnp.maximum(m, jnp.max(s, axis=-1, keepdims=True))
            alpha = jnp.exp(m - m_new)
            p = jnp.exp(s - m_new)
            l_new = alpha * l + jnp.sum(p, axis=-1, keepdims=True)
            acc_new = alpha * acc + _dot(p.astype(BF16), vj)
            new.append((m_new, l_new, acc_new))
        return tuple(new)

    init_h = (jnp.full((tq, 1), -jnp.inf, F32), jnp.zeros((tq, 1), F32),
              jnp.zeros((tq, 2 * FOX_HD), F32))
    carry = lax.fori_loop(0, qi, functools.partial(step, masked=False), (init_h, init_h))
    (_, l0, acc0), (_, l1, acc1) = step(qi, carry, True)
    o_ref[...] = jnp.where(lane < FOX_HD, acc0 / l0, acc1 / l1).astype(BF16)


def _fox(q, k, v, cneg, *, tq=512):
    B, S, _ = q.shape
    n_pairs = FOX_HEADS // 2
    return pl.pallas_call(
        functools.partial(_fox_kernel, tq=tq),
        grid=(B, n_pairs, S // tq),
        in_specs=[
            pl.BlockSpec((None, tq, 2 * FOX_HD), lambda b, p, i: (b, i, p)),
            pl.BlockSpec((None, S, 2 * FOX_HD), lambda b, p, i: (b, 0, p)),
            pl.BlockSpec((None, S, 2 * FOX_HD), lambda b, p, i: (b, 0, p)),
            pl.BlockSpec((None, None, 2, S), lambda b, p, i: (b, p, 0, 0)),
        ],
        out_specs=pl.BlockSpec((None, tq, 2 * FOX_HD), lambda b, p, i: (b, i, p)),
        out_shape=jax.ShapeDtypeStruct((B, S, FOX_W), BF16),
        compiler_params=_params(3),
        name="fox_attention",
    )(q, k, v, cneg)


def _merge_kernel(x_ref, a_ref, b_ref, c_ref, gl_ref, bg_ref,
                  ua_ref, ub_ref, uc_ref, wo_ref, o_ref):
    merged = None
    for n, (br_ref, up_ref) in enumerate(((a_ref, ua_ref), (b_ref, ub_ref), (c_ref, uc_ref))):
        cols = slice(n * D_MODEL, (n + 1) * D_MODEL)
        gate = jax.nn.sigmoid(gl_ref[:, cols] + bg_ref[:, cols])
        term = gate * _dot(br_ref[...], up_ref[...])
        merged = term if merged is None else merged + term
    o_ref[...] = x_ref[...] + _dot(merged.astype(BF16), wo_ref[...])


def _merge(x, ya, yb, yc, gl, b_gate, ua, ub, uc, wo, *, tm=512):
    T = x.shape[0]

    def rows(width):
        return pl.BlockSpec((tm, width), lambda i: (i, 0))

    return pl.pallas_call(
        _merge_kernel,
        grid=(T // tm,),
        in_specs=[
            rows(D_MODEL), rows(POOL_W), rows(LRU_W), rows(FOX_W), rows(N_BRANCH * D_MODEL),
            _const_spec((1, N_BRANCH * D_MODEL)),
            _const_spec((POOL_W, D_MODEL)),
            _const_spec((LRU_W, D_MODEL)),
            _const_spec((FOX_W, D_MODEL)),
            _const_spec((D_MODEL, D_MODEL)),
        ],
        out_specs=rows(D_MODEL),
        out_shape=jax.ShapeDtypeStruct((T, D_MODEL), F32),
        compiler_params=_params(1),
        name="merge",
    )(x, ya, yb, yc, gl, b_gate, ua, ub, uc, wo)


def _memkv_kernel(m_ref, g_ref, w_ref, k_ref, v_ref):
    h = _rms(m_ref[...], g_ref[...]).astype(BF16)
    k_ref[...] = _dot(h, w_ref[:, :D_MODEL]).astype(BF16)
    v_ref[...] = _dot(h, w_ref[:, D_MODEL:]).astype(BF16)


def _memkv(mem, g, w_kv_bf):
    B = mem.shape[0]
    tile = pl.BlockSpec((None, N_MEM, D_MODEL), lambda b: (b, 0, 0))
    return pl.pallas_call(
        _memkv_kernel,
        grid=(B,),
        in_specs=[tile, _const_spec((1, D_MODEL)), _const_spec((D_MODEL, 2 * D_MODEL))],
        out_specs=(tile, tile),
        out_shape=(jax.ShapeDtypeStruct((B, N_MEM, D_MODEL), BF16),) * 2,
        compiler_params=_params(1),
        name="mem_kv",
    )(mem, g, w_kv_bf)


def _cross_kernel(x_ref, g_ref, wq_ref, k_ref, v_ref, wo_ref, o_ref):
    x = x_ref[...]
    h = _rms(x, g_ref[...]).astype(BF16)
    q = (_dot(h, wq_ref[...]) * (X_HD ** -0.5)).astype(BF16)
    heads = []
    for hd in range(X_HEADS):
        cols = slice(hd * X_HD, (hd + 1) * X_HD)
        s = _dot_nt(q[:, cols], k_ref[:, cols])
        e = jnp.exp(s - jnp.max(s, axis=-1, keepdims=True))
        p = e / jnp.sum(e, axis=-1, keepdims=True)
        heads.append(_dot(p.astype(BF16), v_ref[:, cols]).astype(BF16))
    o = jnp.concatenate(heads, axis=-1)
    o_ref[...] = x + _dot(o, wo_ref[...])


def _cross(x, g, wq, k, v, wo, *, tm=512):
    B, S, _ = x.shape
    tile = pl.BlockSpec((None, tm, D_MODEL), lambda b, s: (b, s, 0))
    kv = pl.BlockSpec((None, N_MEM, D_MODEL), lambda b, s: (b, 0, 0))
    return pl.pallas_call(
        _cross_kernel,
        grid=(B, S // tm),
        in_specs=[tile, _const_spec((1, D_MODEL)), _const_spec((D_MODEL, D_MODEL)), kv, kv,
                  _const_spec((D_MODEL, D_MODEL))],
        out_specs=tile,
        out_shape=jax.ShapeDtypeStruct((B, S, D_MODEL), F32),
        compiler_params=_params(2),
        name="cross_attention",
    )(x, g, wq, k, v, wo)


def kernel(x, mem, g_ffn1, w_ffn1_in, w_ffn1_out, g_mix, w_in, b_f, b_gate, w_pool, pool_scale,
           w_up_a, conv_w, conv_b, w_rg_a, b_rg_a, w_rg_x, b_rg_x, lru_lambda, w_up_b, w_up_c, w_o,
           g_cross, g_mem, w_xq, w_xkv, w_xo, g_ffn2, w_ffn2_in, w_ffn2_out, g_final):
    B, S, D = x.shape
    T = B * S
    depth = g_ffn1.shape[0]
    off_fl = OFF_GL
    row = lambda vec: vec.reshape(1, -1)
    g_fin = row(g_final)

    xt = x.reshape(T, D)
    for l in range(depth):
        xt = _ffn(xt, row(g_ffn1[l]), w_ffn1_in[l].astype(BF16), w_ffn1_out[l].astype(BF16),
                  g_fin, final_norm=False)

        w_l = w_in[l]
        w_main = jnp.concatenate([w_l[:, :off_fl], w_l[:, off_fl + FOX_HEADS:]], axis=1).astype(BF16)
        w_fl_t = w_l[:, off_fl:off_fl + FOX_HEADS].T.astype(BF16)
        xa, xb, gb, q, k, v, gl, fl_t = _inproj(xt, row(g_mix[l]), w_main, w_fl_t)

        ya = _pool(xa.reshape(B, S, POOL_W), w_pool[l].astype(BF16), row(pool_scale[l]))
        wg = jnp.concatenate([w_rg_a[l], w_rg_x[l]], axis=-1).astype(BF16)
        yb = _rglru(xb.reshape(B, S, LRU_W), gb.reshape(B, S, LRU_W), conv_w[l], row(conv_b[l]),
                    wg, row(b_rg_a[l]), row(b_rg_x[l]), row(lru_lambda[l]))
        cneg = _logf_cumsum(fl_t, b_f[l].reshape(FOX_HEADS, 1), seq=S)
        cneg = cneg.reshape(FOX_HEADS // 2, 2, B, S).transpose(2, 0, 1, 3)
        yc = _fox(q.reshape(B, S, FOX_W), k.reshape(B, S, FOX_W), v.reshape(B, S, FOX_W), cneg)

        xt = _merge(xt, ya.reshape(T, POOL_W), yb.reshape(T, LRU_W), yc.reshape(T, FOX_W), gl,
                    row(b_gate[l]), w_up_a[l].astype(BF16), w_up_b[l].astype(BF16),
                    w_up_c[l].astype(BF16), w_o[l].astype(BF16))

        mk, mv = _memkv(mem, row(g_mem[l]), w_xkv[l].astype(BF16))
        xt = _cross(xt.reshape(B, S, D), row(g_cross[l]), w_xq[l].astype(BF16), mk, mv,
                    w_xo[l].astype(BF16)).reshape(T, D)

        xt = _ffn(xt, row(g_ffn2[l]), w_ffn2_in[l].astype(BF16), w_ffn2_out[l].astype(BF16),
                  g_fin, final_norm=(l == depth - 1))
    return xt.reshape(B, S, D)
```

```python
import functools

import jax
import jax.numpy as jnp
from jax import lax
from jax.experimental import pallas as pl
from jax.experimental.pallas import tpu as pltpu

F32 = jnp.float32
BF16 = jnp.bfloat16

D_MODEL = 1024
N_MEM = 256
POOL_WINDOWS = (2, 4, 8, 16)
POOL_GROUPS = 4
POOL_W = 512
POOL_GW = 128
POOL_HALO = 16
LRU_W = 1024
LRU_HEADS = 8
LRU_HD = 128
CONV_W = 4
CONV_HALO = 8
LRU_C = 8.0
FOX_HEADS = 8
FOX_HD = 64
FOX_W = 512
X_HEADS = 4
X_HD = 256
D_FF = 2816
N_BRANCH = 3
EPS = 1e-6
SUBLANES = 8
LANES = 128
VMEM_LIMIT = 56 * 1024 * 1024
NEG_BIG = -1e30

IN_XA = 0
IN_XB = IN_XA + POOL_W
IN_GB = IN_XB + LRU_W
IN_Q = IN_GB + LRU_W
IN_K = IN_Q + FOX_W
IN_V = IN_K + FOX_W
IN_FL = IN_V + FOX_W
IN_GL = IN_FL + FOX_HEADS
OFF_XA = 0
OFF_XB = OFF_XA + POOL_W
OFF_GB = OFF_XB + LRU_W
OFF_K = OFF_GB + LRU_W
OFF_GL = OFF_K + FOX_W
IN_MAIN = OFF_GL + N_BRANCH * D_MODEL
KAUG_W = 128
N_SPLIT = 3
VAUG_H = FOX_HD + 16


def _params(n_grid_dims):
    return pltpu.CompilerParams(
        dimension_semantics=("arbitrary",) * n_grid_dims,
        vmem_limit_bytes=VMEM_LIMIT,
    )


def _const_spec(shape):
    nd = len(shape)
    return pl.BlockSpec(shape, lambda *_: (0,) * nd, pipeline_mode=pl.Buffered(1))


def _rms(x, g):
    ms = jnp.mean(x * x, axis=-1, keepdims=True)
    return x * lax.rsqrt(ms + EPS) * g


def _softplus(z):
    return jnp.maximum(z, 0.0) + jnp.log1p(jnp.exp(-jnp.abs(z)))


def _dot(a, b):
    return jnp.dot(a, b, preferred_element_type=F32)


def _dot_nt(a, b):
    return lax.dot_general(a, b, (((1,), (1,)), ((), ())), preferred_element_type=F32)


def _ffn_kernel(x_ref, g_ref, wa_ref, wb_ref, wo_ref, gf_ref, o_ref, *, n_chunks, final_norm):
    x = x_ref[...]
    h = _rms(x, g_ref[...]).astype(BF16)
    fc = D_FF // n_chunks
    acc = None
    for c in range(n_chunks):
        a = _dot(h, wa_ref[:, c * fc:(c + 1) * fc])
        b = _dot(h, wb_ref[:, c * fc:(c + 1) * fc])
        act = (a * jax.nn.sigmoid(a) * b).astype(BF16)
        part = _dot(act, wo_ref[c * fc:(c + 1) * fc, :])
        acc = part if acc is None else acc + part
    y = x + 0.5 * acc
    if final_norm:
        y = _rms(y, gf_ref[...])
    o_ref[...] = y


def _ffn(x, g, w_in_bf, w_out_bf, g_final, *, final_norm, tm=512, n_chunks=2):
    T = x.shape[0]
    return pl.pallas_call(
        functools.partial(_ffn_kernel, n_chunks=n_chunks, final_norm=final_norm),
        grid=(T // tm,),
        in_specs=[
            pl.BlockSpec((tm, D_MODEL), lambda i: (i, 0)),
            _const_spec((1, D_MODEL)),
            pl.BlockSpec((D_MODEL, D_FF), lambda i: (0, 0), pipeline_mode=pl.Buffered(1)),
            pl.BlockSpec((D_MODEL, D_FF), lambda i: (0, 1), pipeline_mode=pl.Buffered(1)),
            _const_spec((D_FF, D_MODEL)),
            _const_spec((1, D_MODEL)),
        ],
        out_specs=pl.BlockSpec((tm, D_MODEL), lambda i: (i, 0)),
        out_shape=jax.ShapeDtypeStruct((T, D_MODEL), F32),
        compiler_params=_params(1),
        name="ffn",
    )(x, g, w_in_bf, w_in_bf, w_out_bf, g_final)


def _inproj_kernel(x_ref, g_ref, w_ref, wqv_ref, wfl_ref, bg_ref,
                   xa_ref, xb_ref, gb_ref, k_ref, gate_ref, qt_ref, vt_ref, fl_ref):
    h = _rms(x_ref[...], g_ref[...]).astype(BF16)

    def seg(lo, width):
        return _dot(h, w_ref[:, lo:lo + width])

    xa_ref[...] = seg(OFF_XA, POOL_W)
    xb_ref[...] = seg(OFF_XB, LRU_W)
    gb_ref[...] = seg(OFF_GB, LRU_W)
    k_ref[...] = seg(OFF_K, FOX_W).astype(BF16)
    gate_ref[...] = jax.nn.sigmoid(seg(OFF_GL, N_BRANCH * D_MODEL) + bg_ref[...]).astype(BF16)
    qv_t = _dot_nt(wqv_ref[...], h)
    qt_ref[...] = (qv_t[:FOX_W] * (FOX_HD ** -0.5)).astype(BF16)
    vt_ref[...] = qv_t[FOX_W:].astype(BF16)
    fl_ref[...] = _dot_nt(wfl_ref[...], h)


def _inproj(x, g, w_main_bf, w_qv_t_bf, w_fl_t_bf, b_gate, *, tm=256):
    T = x.shape[0]

    def rows(width):
        return pl.BlockSpec((tm, width), lambda i: (i, 0))

    def cols(height):
        return pl.BlockSpec((height, tm), lambda i: (0, i))

    out_shape = (
        jax.ShapeDtypeStruct((T, POOL_W), F32),
        jax.ShapeDtypeStruct((T, LRU_W), F32),
        jax.ShapeDtypeStruct((T, LRU_W), F32),
        jax.ShapeDtypeStruct((T, FOX_W), BF16),
        jax.ShapeDtypeStruct((T, N_BRANCH * D_MODEL), BF16),
        jax.ShapeDtypeStruct((FOX_W, T), BF16),
        jax.ShapeDtypeStruct((FOX_W, T), BF16),
        jax.ShapeDtypeStruct((FOX_HEADS, T), F32),
    )
    out_specs = (
        rows(POOL_W), rows(LRU_W), rows(LRU_W), rows(FOX_W), rows(N_BRANCH * D_MODEL),
        cols(FOX_W), cols(FOX_W), cols(FOX_HEADS),
    )
    return pl.pallas_call(
        _inproj_kernel,
        grid=(T // tm,),
        in_specs=[
            rows(D_MODEL),
            _const_spec((1, D_MODEL)),
            _const_spec((D_MODEL, IN_MAIN)),
            _const_spec((2 * FOX_W, D_MODEL)),
            _const_spec((FOX_HEADS, D_MODEL)),
            _const_spec((1, N_BRANCH * D_MODEL)),
        ],
        out_specs=out_specs,
        out_shape=out_shape,
        compiler_params=_params(1),
        name="inproj",
    )(x, g, w_main_bf, w_qv_t_bf, w_fl_t_bf, b_gate)


def _pool_kernel(xa_ref, wp_ref, sc_ref, o_ref, ext_ref, *, ts):
    s = pl.program_id(1)

    @pl.when(s == 0)
    def _():
        ext_ref[0:POOL_HALO, :] = jnp.zeros((POOL_HALO, POOL_W), F32)

    x = xa_ref[...]
    ext_ref[POOL_HALO:POOL_HALO + ts, :] = x
    e = ext_ref[...]
    ext_ref[0:POOL_HALO, :] = e[ts:ts + POOL_HALO, :]

    sums = []
    cur = e
    shift = 1
    for g in range(POOL_GROUPS):
        cur = cur + pltpu.roll(cur, shift, 0)
        sums.append(cur[POOL_HALO:, 0:POOL_GW])
        if g + 1 < POOL_GROUPS:
            cur = cur[:, POOL_GW:]
        shift *= 2

    pos = (s * ts + 1 + lax.broadcasted_iota(jnp.int32, (ts, 1), 0)).astype(F32)
    for g, w in enumerate(POOL_WINDOWS):
        inv_cnt = 1.0 / jnp.minimum(pos, float(w))
        mean = sums[g] * inv_cnt
        d = (mean - x[:, g * POOL_GW:(g + 1) * POOL_GW]).astype(BF16)
        y = _dot(d, wp_ref[g]) * sc_ref[:, g * POOL_GW:(g + 1) * POOL_GW]
        o_ref[:, g * POOL_GW:(g + 1) * POOL_GW] = y.astype(BF16)


def _pool(xa, w_pool_bf, scale, *, ts=512):
    B, S, _ = xa.shape
    return pl.pallas_call(
        functools.partial(_pool_kernel, ts=ts),
        grid=(B, S // ts),
        in_specs=[
            pl.BlockSpec((None, ts, POOL_W), lambda b, s: (b, s, 0)),
            _const_spec((POOL_GROUPS, POOL_GW, POOL_GW)),
            _const_spec((1, POOL_W)),
        ],
        out_specs=pl.BlockSpec((None, ts, POOL_W), lambda b, s: (b, s, 0)),
        out_shape=jax.ShapeDtypeStruct((B, S, POOL_W), BF16),
        scratch_shapes=[pltpu.VMEM((ts + POOL_HALO, POOL_W), F32)],
        compiler_params=_params(2),
        name="pool",
    )(xa, w_pool_bf, scale)


def _rglru_kernel(xb_ref, gb_ref, cw_ref, cb_ref, wg_ref, ba_ref, bx_ref, lam_ref, o_ref,
                  ext_ref, a_ref, u_ref, hc_ref, *, ts, lane_chunk):
    s = pl.program_id(1)

    @pl.when(s == 0)
    def _():
        ext_ref[0:CONV_HALO, :] = jnp.zeros((CONV_HALO, LRU_W), F32)
        hc_ref[...] = jnp.zeros((SUBLANES, LRU_W), F32)

    ext_ref[CONV_HALO:CONV_HALO + ts, :] = xb_ref[...]
    ext = ext_ref[...]
    y = cb_ref[...]
    for k in range(CONV_W):
        lag = CONV_W - 1 - k
        shifted = pltpu.roll(ext, lag, 0) if lag else ext
        y = y + shifted[CONV_HALO:, :] * cw_ref[k:k + 1, :]
    ext_ref[0:CONV_HALO, :] = ext[ts:ts + CONV_HALO, :]

    yb = y.astype(BF16)
    for h in range(LRU_HEADS):
        cols = slice(h * LRU_HD, (h + 1) * LRU_HD)
        gates = _dot(yb[:, cols], wg_ref[h])
        r = jax.nn.sigmoid(gates[:, :LRU_HD] + ba_ref[:, cols])
        i = jax.nn.sigmoid(gates[:, LRU_HD:] + bx_ref[:, cols])
        log_a = -LRU_C * r * _softplus(-lam_ref[:, cols])
        a = jnp.exp(log_a)
        one_minus_a2 = -jnp.tanh(log_a) * (a * a + 1.0)
        a_ref[:, cols] = a
        u_ref[:, cols] = jnp.sqrt(one_minus_a2) * (i * y[:, cols])

    row = lax.broadcasted_iota(jnp.int32, (SUBLANES, lane_chunk), 0)
    for c in range(LRU_W // lane_chunk):
        cols = slice(c * lane_chunk, (c + 1) * lane_chunk)

        def group(gi, h_prev, cols=cols):
            off = pl.multiple_of(gi * SUBLANES, SUBLANES)
            a = a_ref[pl.ds(off, SUBLANES), cols]
            u = u_ref[pl.ds(off, SUBLANES), cols]
            for d in (1, 2, 4):
                a_sh = jnp.where(row >= d, pltpu.roll(a, d, 0), 1.0)
                u_sh = jnp.where(row >= d, pltpu.roll(u, d, 0), 0.0)
                u = a * u_sh + u
                a = a * a_sh
            hh = a * h_prev + u
            u_ref[pl.ds(off, SUBLANES), cols] = hh
            return jnp.broadcast_to(hh[SUBLANES - 1:SUBLANES, :], (SUBLANES, lane_chunk))

        hc_ref[:, cols] = lax.fori_loop(0, ts // SUBLANES, group, hc_ref[:, cols], unroll=2)

    o_ref[...] = (u_ref[...] * jax.nn.gelu(gb_ref[...])).astype(BF16)


def _rglru(xb, gb, conv_w, conv_b, wg_bf, b_a, b_x, lam, *, ts=256, lane_chunk=1024):
    B, S, _ = xb.shape
    tile = pl.BlockSpec((None, ts, LRU_W), lambda b, s: (b, s, 0))
    return pl.pallas_call(
        functools.partial(_rglru_kernel, ts=ts, lane_chunk=lane_chunk),
        grid=(B, S // ts),
        in_specs=[
            tile, tile,
            _const_spec((CONV_W, LRU_W)),
            _const_spec((1, LRU_W)),
            _const_spec((LRU_HEADS, LRU_HD, 2 * LRU_HD)),
            _const_spec((1, LRU_W)),
            _const_spec((1, LRU_W)),
            _const_spec((1, LRU_W)),
        ],
        out_specs=tile,
        out_shape=jax.ShapeDtypeStruct((B, S, LRU_W), BF16),
        scratch_shapes=[
            pltpu.VMEM((ts + CONV_HALO, LRU_W), F32),
            pltpu.VMEM((ts, LRU_W), F32),
            pltpu.VMEM((ts, LRU_W), F32),
            pltpu.VMEM((SUBLANES, LRU_W), F32),
        ],
        compiler_params=_params(2),
        name="rglru",
    )(xb, gb, conv_w, conv_b, wg_bf, b_a, b_x, lam)


def _logf_cumsum_kernel(fl_ref, bf_ref, o_ref, *, seq):
    z = fl_ref[...] + bf_ref[...]
    x = jnp.minimum(z, 0.0) - jnp.log1p(jnp.exp(-jnp.abs(z)))
    lane = lax.broadcasted_iota(jnp.int32, x.shape, 1)
    d = 1
    while d < seq:
        x = x + jnp.where(lane >= d, pltpu.roll(x, d, 1), 0.0)
        d *= 2
    rest = -x
    for n in range(N_SPLIT):
        piece = rest.astype(BF16)
        o_ref[n] = piece
        rest = rest - piece.astype(F32)


def _logf_cumsum(fl_t, b_f, *, seq):
    H, T = fl_t.shape
    return pl.pallas_call(
        functools.partial(_logf_cumsum_kernel, seq=seq),
        grid=(T // seq,),
        in_specs=[
            pl.BlockSpec((H, seq), lambda b: (0, b)),
            _const_spec((H, 1)),
        ],
        out_specs=pl.BlockSpec((N_SPLIT, H, seq), lambda b: (0, 0, b)),
        out_shape=jax.ShapeDtypeStruct((N_SPLIT, H, T), BF16),
        compiler_params=_params(1),
        name="logf_cumsum",
    )(fl_t, b_f)


def _max_over_rows(s):
    while s.shape[0] % (2 * SUBLANES) == 0:
        half = s.shape[0] // 2
        s = jnp.maximum(s[:half], s[half:])
    return jnp.max(s, axis=0, keepdims=True)


def _fox_kernel(qt_ref, kaug_ref, vt_ref, o_ref, m_ref, acc_ref, s_even_ref, s_odd_ref,
                *, tq, tk, strip):
    qi = pl.program_id(2)
    n_strips = tq // strip
    aug_row = lax.broadcasted_iota(jnp.int32, (KAUG_W - FOX_HD, strip), 0)
    ones_rows = jnp.where(aug_row < N_SPLIT, 1.0, 0.0).astype(BF16)
    q_aug = [jnp.concatenate([qt_ref[:, c * strip:(c + 1) * strip], ones_rows], axis=0)
             for c in range(n_strips)]

    def scores(off, width, c):
        return _dot(kaug_ref[pl.ds(off, width), :], q_aug[c])

    def update(off, width, c, s, first_query=None):
        if first_query is not None:
            key = lax.broadcasted_iota(jnp.int32, (width, strip), 0)
            qry = lax.broadcasted_iota(jnp.int32, (width, strip), 1) + first_query
            s = jnp.where(key <= qry, s, NEG_BIG)
        m = m_ref[c]
        m_new = jnp.maximum(m, _max_over_rows(s))
        alpha = jnp.exp(m - m_new)
        p = jnp.exp(s - m_new).astype(BF16)
        m_ref[c] = m_new
        ones_row = jnp.where(lax.broadcasted_iota(jnp.int32, (VAUG_H - FOX_HD, width), 0) == 0,
                             1.0, 0.0).astype(BF16)
        v_aug = jnp.concatenate([vt_ref[:, pl.ds(off, width)], ones_row], axis=0)
        acc_ref[c] = alpha * acc_ref[c] + _dot(v_aug, p)

    for c in range(n_strips):
        m_ref[c] = jnp.full((1, strip), -jnp.inf, F32)
        acc_ref[c] = jnp.zeros((VAUG_H, strip), F32)

    n_bulk = qi * (tq // tk)
    early = range(0, n_strips // 2)
    late = range(n_strips // 2, n_strips)

    def late_update(j, s_ref):
        off = pl.multiple_of(j * tk, tk)
        for c in late:
            update(off, tk, c, s_ref[c - late.start])

    def stage(j, read_ref, write_ref):
        off = pl.multiple_of(j * tk, tk)
        s_early = [scores(off, tk, c) for c in early]
        if read_ref is not None:
            late_update(j - 1, read_ref)
        for c in late:
            write_ref[c - late.start] = scores(off, tk, c)
        for n, c in enumerate(early):
            update(off, tk, c, s_early[n])

    @pl.when(n_bulk > 0)
    def _():
        stage(0, None, s_even_ref)

    def two_stages(jj, carry):
        stage(2 * jj + 1, s_even_ref, s_odd_ref)
        stage(2 * jj + 2, s_odd_ref, s_even_ref)
        return carry

    lax.fori_loop(0, (n_bulk - 1) // 2, two_stages, 0)
    last = n_bulk - 1
    last_is_odd = last % 2 == 1

    @pl.when(jnp.logical_and(n_bulk > 0, last_is_odd))
    def _():
        stage(last, s_even_ref, s_odd_ref)
        late_update(last, s_odd_ref)

    @pl.when(jnp.logical_and(n_bulk > 0, jnp.logical_not(last_is_odd)))
    def _():
        late_update(last, s_even_ref)

    base = pl.multiple_of(qi * tq, tq)
    s_diag = [scores(base, (c + 1) * strip, c) for c in range(n_strips)]
    for c in range(n_strips):
        update(base, (c + 1) * strip, c, s_diag[c], first_query=c * strip)
        acc = acc_ref[c]
        o_ref[:, c * strip:(c + 1) * strip] = (
            acc[:FOX_HD] / acc[FOX_HD:FOX_HD + 1]).astype(BF16)


def _fox(q_t, kaug, v_t, *, seq, tq=1024, tk=1024, strip=256):
    B, S = kaug.shape[0], seq
    nq = S // tq
    n_strips = tq // strip
    return pl.pallas_call(
        functools.partial(_fox_kernel, tq=tq, tk=tk, strip=strip),
        grid=(B, FOX_HEADS, nq),
        in_specs=[
            pl.BlockSpec((FOX_HD, tq), lambda b, h, i: (h, b * nq + i)),
            pl.BlockSpec((None, S, KAUG_W), lambda b, h, i: (b, 0, h)),
            pl.BlockSpec((FOX_HD, S), lambda b, h, i: (h, b)),
        ],
        out_specs=pl.BlockSpec((FOX_HD, tq), lambda b, h, i: (h, b * nq + i)),
        out_shape=jax.ShapeDtypeStruct((FOX_W, B * S), BF16),
        scratch_shapes=[
            pltpu.VMEM((n_strips, 1, strip), F32),
            pltpu.VMEM((n_strips, VAUG_H, strip), F32),
            pltpu.VMEM((n_strips // 2, tk, strip), F32),
            pltpu.VMEM((n_strips // 2, tk, strip), F32),
        ],
        compiler_params=_params(3),
        name="fox_attention",
    )(q_t, kaug, v_t)


def _merge_kernel(x_ref, a_ref, b_ref, c_ref, gate_ref,
                  ua_ref, ub_ref, uc_ref, wo_ref, o_ref):
    merged = None
    for n, (br_ref, up_ref) in enumerate(((a_ref, ua_ref), (b_ref, ub_ref), (c_ref, uc_ref))):
        gate = gate_ref[:, n * D_MODEL:(n + 1) * D_MODEL].astype(F32)
        term = gate * _dot(br_ref[...], up_ref[...])
        merged = term if merged is None else merged + term
    o_ref[...] = x_ref[...] + _dot(merged.astype(BF16), wo_ref[...])


def _merge(x, ya, yb, yc, gate, ua, ub, uc, wo, *, tm=512):
    T = x.shape[0]

    def rows(width):
        return pl.BlockSpec((tm, width), lambda i: (i, 0))

    return pl.pallas_call(
        _merge_kernel,
        grid=(T // tm,),
        in_specs=[
            rows(D_MODEL), rows(POOL_W), rows(LRU_W), rows(FOX_W), rows(N_BRANCH * D_MODEL),
            _const_spec((POOL_W, D_MODEL)),
            _const_spec((LRU_W, D_MODEL)),
            _const_spec((FOX_W, D_MODEL)),
            _const_spec((D_MODEL, D_MODEL)),
        ],
        out_specs=rows(D_MODEL),
        out_shape=jax.ShapeDtypeStruct((T, D_MODEL), F32),
        compiler_params=_params(1),
        name="merge",
    )(x, ya, yb, yc, gate, ua, ub, uc, wo)


def _memkv_kernel(m_ref, g_ref, w_ref, k_ref, v_ref):
    h = _rms(m_ref[...], g_ref[...]).astype(BF16)
    k_ref[...] = _dot(h, w_ref[:, :D_MODEL]).astype(BF16)
    v_ref[...] = _dot(h, w_ref[:, D_MODEL:]).astype(BF16)


def _memkv(mem, g, w_kv_bf):
    B = mem.shape[0]
    tile = pl.BlockSpec((None, N_MEM, D_MODEL), lambda b: (b, 0, 0))
    return pl.pallas_call(
        _memkv_kernel,
        grid=(B,),
        in_specs=[tile, _const_spec((1, D_MODEL)), _const_spec((D_MODEL, 2 * D_MODEL))],
        out_specs=(tile, tile),
        out_shape=(jax.ShapeDtypeStruct((B, N_MEM, D_MODEL), BF16),) * 2,
        compiler_params=_params(1),
        name="mem_kv",
    )(mem, g, w_kv_bf)


def _cross_kernel(x_ref, g_ref, wq_ref, k_ref, v_ref, wo_ref, o_ref):
    x = x_ref[...]
    h = _rms(x, g_ref[...]).astype(BF16)
    q = (_dot(h, wq_ref[...]) * (X_HD ** -0.5)).astype(BF16)
    heads = []
    for hd in range(X_HEADS):
        cols = slice(hd * X_HD, (hd + 1) * X_HD)
        s = _dot_nt(q[:, cols], k_ref[:, cols])
        e = jnp.exp(s - jnp.max(s, axis=-1, keepdims=True))
        p = e / jnp.sum(e, axis=-1, keepdims=True)
        heads.append(_dot(p.astype(BF16), v_ref[:, cols]).astype(BF16))
    o = jnp.concatenate(heads, axis=-1)
    o_ref[...] = x + _dot(o, wo_ref[...])


def _cross(x, g, wq, k, v, wo, *, tm=512):
    B, S, _ = x.shape
    tile = pl.BlockSpec((None, tm, D_MODEL), lambda b, s: (b, s, 0))
    kv = pl.BlockSpec((None, N_MEM, D_MODEL), lambda b, s: (b, 0, 0))
    return pl.pallas_call(
        _cross_kernel,
        grid=(B, S // tm),
        in_specs=[tile, _const_spec((1, D_MODEL)), _const_spec((D_MODEL, D_MODEL)), kv, kv,
                  _const_spec((D_MODEL, D_MODEL))],
        out_specs=tile,
        out_shape=jax.ShapeDtypeStruct((B, S, D_MODEL), F32),
        compiler_params=_params(2),
        name="cross_attention",
    )(x, g, wq, k, v, wo)


def kernel(x, mem, g_ffn1, w_ffn1_in, w_ffn1_out, g_mix, w_in, b_f, b_gate, w_pool, pool_scale,
           w_up_a, conv_w, conv_b, w_rg_a, b_rg_a, w_rg_x, b_rg_x, lru_lambda, w_up_b, w_up_c, w_o,
           g_cross, g_mem, w_xq, w_xkv, w_xo, g_ffn2, w_ffn2_in, w_ffn2_out, g_final):
    B, S, D = x.shape
    T = B * S
    depth = g_ffn1.shape[0]
    row = lambda vec: vec.reshape(1, -1)
    g_fin = row(g_final)

    xt = x.reshape(T, D)
    for l in range(depth):
        xt = _ffn(xt, row(g_ffn1[l]), w_ffn1_in[l].astype(BF16), w_ffn1_out[l].astype(BF16),
                  g_fin, final_norm=False)

        w_l = w_in[l]
        w_main = jnp.concatenate([w_l[:, IN_XA:IN_Q], w_l[:, IN_K:IN_V], w_l[:, IN_GL:]],
                                 axis=1).astype(BF16)
        w_qv_t = jnp.concatenate([w_l[:, IN_Q:IN_K], w_l[:, IN_V:IN_FL]], axis=1).T.astype(BF16)
        w_fl_t = w_l[:, IN_FL:IN_GL].T.astype(BF16)
        xa, xb, gb, k, gate, q_t, v_t, fl_t = _inproj(xt, row(g_mix[l]), w_main, w_qv_t, w_fl_t,
                                                      row(b_gate[l]))

        ya = _pool(xa.reshape(B, S, POOL_W), w_pool[l].astype(BF16), row(pool_scale[l]))
        wg = jnp.concatenate([w_rg_a[l], w_rg_x[l]], axis=-1).astype(BF16)
        yb = _rglru(xb.reshape(B, S, LRU_W), gb.reshape(B, S, LRU_W), conv_w[l], row(conv_b[l]),
                    wg, row(b_rg_a[l]), row(b_rg_x[l]), row(lru_lambda[l]))
        cneg = _logf_cumsum(fl_t, b_f[l].reshape(FOX_HEADS, 1), seq=S)
        kaug = jnp.concatenate(
            [k.reshape(T, FOX_HEADS, FOX_HD), cneg.transpose(2, 1, 0),
             jnp.zeros((T, FOX_HEADS, KAUG_W - FOX_HD - N_SPLIT), BF16)],
            axis=-1).reshape(B, S, FOX_HEADS * KAUG_W)
        yc = _fox(q_t, kaug, v_t, seq=S).T

        xt = _merge(xt, ya.reshape(T, POOL_W), yb.reshape(T, LRU_W), yc, gate,
                    w_up_a[l].astype(BF16), w_up_b[l].astype(BF16),
                    w_up_c[l].astype(BF16), w_o[l].astype(BF16))

        mk, mv = _memkv(mem, row(g_mem[l]), w_xkv[l].astype(BF16))
        xt = _cross(xt.reshape(B, S, D), row(g_cross[l]), w_xq[l].astype(BF16), mk, mv,
                    w_xo[l].astype(BF16)).reshape(T, D)

        xt = _ffn(xt, row(g_ffn2[l]), w_ffn2_in[l].astype(BF16), w_ffn2_out[l].astype(BF16),
                  g_fin, final_norm=(l == depth - 1))
    return xt.reshape(B, S, D)
```

```python
import functools

import jax
import jax.numpy as jnp
from jax import lax
from jax.experimental import pallas as pl
from jax.experimental.pallas import tpu as pltpu

F32 = jnp.float32
BF16 = jnp.bfloat16

D_MODEL = 1024
N_MEM = 256
POOL_WINDOWS = (2, 4, 8, 16)
POOL_GROUPS = 4
POOL_W = 512
POOL_GW = 128
POOL_HALO = 16
LRU_W = 1024
LRU_HEADS = 8
LRU_HD = 128
CONV_W = 4
CONV_HALO = 8
LRU_C = 8.0
FOX_HEADS = 8
FOX_HD = 64
FOX_W = 512
X_HEADS = 4
X_HD = 256
D_FF = 2816
N_BRANCH = 3
EPS = 1e-6
SUBLANES = 8
BF16_ROWS = 16
VMEM_LIMIT = 56 * 1024 * 1024
NEG_BIG = -1e30

IN_XA = 0
IN_XB = IN_XA + POOL_W
IN_GB = IN_XB + LRU_W
IN_Q = IN_GB + LRU_W
IN_K = IN_Q + FOX_W
IN_V = IN_K + FOX_W
IN_FL = IN_V + FOX_W
IN_GL = IN_FL + FOX_HEADS
OFF_XA = 0
OFF_XB = OFF_XA + POOL_W
OFF_GB = OFF_XB + LRU_W
OFF_GL = OFF_GB + LRU_W
IN_MAIN = OFF_GL + N_BRANCH * D_MODEL
N_SPLIT = 3
AUG_H = FOX_HD + BF16_ROWS


def _params(n_grid_dims):
    return pltpu.CompilerParams(
        dimension_semantics=("arbitrary",) * n_grid_dims,
        vmem_limit_bytes=VMEM_LIMIT,
    )


def _layer_spec(arr, layer):
    shape = arr.shape[1:]
    zeros = (0,) * len(shape)
    return pl.BlockSpec((None,) + shape, lambda *_: (layer,) + zeros,
                        pipeline_mode=pl.Buffered(1))


def _rms(x, g):
    ms = jnp.mean(x * x, axis=-1, keepdims=True)
    return x * lax.rsqrt(ms + EPS) * g


def _softplus(z):
    return jnp.maximum(z, 0.0) + jnp.log1p(jnp.exp(-jnp.abs(z)))


def _dot(a, b):
    return jnp.dot(a, b, preferred_element_type=F32)


def _dot_tn(a, b):
    return lax.dot_general(a, b, (((0,), (0,)), ((), ())), preferred_element_type=F32)


def _dot_nt(a, b):
    return lax.dot_general(a, b, (((1,), (1,)), ((), ())), preferred_element_type=F32)


def _ffn_kernel(x_ref, g_ref, wa_ref, wb_ref, wo_ref, gf_ref, o_ref, *, n_chunks, final_norm):
    x = x_ref[...]
    h = _rms(x, g_ref[...]).astype(BF16)
    fc = D_FF // n_chunks
    acc = None
    for c in range(n_chunks):
        a = _dot(h, wa_ref[:, c * fc:(c + 1) * fc])
        b = _dot(h, wb_ref[:, c * fc:(c + 1) * fc])
        act = (a * jax.nn.sigmoid(a) * b).astype(BF16)
        part = _dot(act, wo_ref[c * fc:(c + 1) * fc, :])
        acc = part if acc is None else acc + part
    y = x + 0.5 * acc
    if final_norm:
        y = _rms(y, gf_ref[...])
    o_ref[...] = y


def _ffn(x, layer, g, w_in_bf, w_out_bf, g_final, *, final_norm, tm=512, n_chunks=2):
    T = x.shape[0]

    def half(j):
        return pl.BlockSpec((None, D_MODEL, D_FF), lambda i: (layer, 0, j),
                            pipeline_mode=pl.Buffered(1))

    return pl.pallas_call(
        functools.partial(_ffn_kernel, n_chunks=n_chunks, final_norm=final_norm),
        grid=(T // tm,),
        in_specs=[
            pl.BlockSpec((tm, D_MODEL), lambda i: (i, 0)),
            _layer_spec(g, layer), half(0), half(1), _layer_spec(w_out_bf, layer),
            _layer_spec(g_final, 0),
        ],
        out_specs=pl.BlockSpec((tm, D_MODEL), lambda i: (i, 0)),
        out_shape=jax.ShapeDtypeStruct((T, D_MODEL), F32),
        compiler_params=_params(1),
        name="ffn",
    )(x, g, w_in_bf, w_in_bf, w_out_bf, g_final)


def _inproj_kernel(x_ref, g_ref, w_ref, wqkv_ref, wfl_ref, bg_ref,
                   xa_ref, xb_ref, gb_ref, gate_ref, qt_ref, kt_ref, vt_ref, fl_ref):
    h = _rms(x_ref[...], g_ref[...]).astype(BF16)

    def seg(lo, width):
        return _dot(h, w_ref[:, lo:lo + width])

    xa_ref[...] = seg(OFF_XA, POOL_W)
    xb_ref[...] = seg(OFF_XB, LRU_W)
    gb_ref[...] = seg(OFF_GB, LRU_W)
    gate_ref[...] = jax.nn.sigmoid(seg(OFF_GL, N_BRANCH * D_MODEL) + bg_ref[...]).astype(BF16)
    qkv_t = _dot_nt(wqkv_ref[...], h)
    qt_ref[...] = (qkv_t[:FOX_W] * (FOX_HD ** -0.5)).astype(BF16)
    kt_ref[...] = qkv_t[FOX_W:2 * FOX_W].astype(BF16)
    vt_ref[...] = qkv_t[2 * FOX_W:].astype(BF16)
    fl_ref[...] = _dot_nt(wfl_ref[...], h)


def _inproj(x, layer, g, w_main_bf, w_qkv_t_bf, w_fl_t_bf, b_gate, *, tm=256):
    T = x.shape[0]

    def rows(width):
        return pl.BlockSpec((tm, width), lambda i: (i, 0))

    def cols(height):
        return pl.BlockSpec((height, tm), lambda i: (0, i))

    out_shape = (
        jax.ShapeDtypeStruct((T, POOL_W), F32),
        jax.ShapeDtypeStruct((T, LRU_W), F32),
        jax.ShapeDtypeStruct((T, LRU_W), F32),
        jax.ShapeDtypeStruct((T, N_BRANCH * D_MODEL), BF16),
        jax.ShapeDtypeStruct((FOX_W, T), BF16),
        jax.ShapeDtypeStruct((FOX_W, T), BF16),
        jax.ShapeDtypeStruct((FOX_W, T), BF16),
        jax.ShapeDtypeStruct((FOX_HEADS, T), F32),
    )
    out_specs = (
        rows(POOL_W), rows(LRU_W), rows(LRU_W), rows(N_BRANCH * D_MODEL),
        cols(FOX_W), cols(FOX_W), cols(FOX_W), cols(FOX_HEADS),
    )
    return pl.pallas_call(
        _inproj_kernel,
        grid=(T // tm,),
        in_specs=[
            rows(D_MODEL),
            _layer_spec(g, layer), _layer_spec(w_main_bf, layer), _layer_spec(w_qkv_t_bf, layer),
            _layer_spec(w_fl_t_bf, layer), _layer_spec(b_gate, layer),
        ],
        out_specs=out_specs,
        out_shape=out_shape,
        compiler_params=_params(1),
        name="inproj",
    )(x, g, w_main_bf, w_qkv_t_bf, w_fl_t_bf, b_gate)


def _pool_kernel(xa_ref, wp_ref, sc_ref, o_ref, ext_ref, *, ts):
    s = pl.program_id(1)

    @pl.when(s == 0)
    def _():
        ext_ref[0:POOL_HALO, :] = jnp.zeros((POOL_HALO, POOL_W), F32)

    x = xa_ref[...]
    ext_ref[POOL_HALO:POOL_HALO + ts, :] = x
    e = ext_ref[...]
    ext_ref[0:POOL_HALO, :] = e[ts:ts + POOL_HALO, :]

    sums = []
    cur = e
    shift = 1
    for g in range(POOL_GROUPS):
        cur = cur + pltpu.roll(cur, shift, 0)
        sums.append(cur[POOL_HALO:, 0:POOL_GW])
        if g + 1 < POOL_GROUPS:
            cur = cur[:, POOL_GW:]
        shift *= 2

    pos = (s * ts + 1 + lax.broadcasted_iota(jnp.int32, (ts, 1), 0)).astype(F32)
    for g, w in enumerate(POOL_WINDOWS):
        inv_cnt = 1.0 / jnp.minimum(pos, float(w))
        mean = sums[g] * inv_cnt
        d = (mean - x[:, g * POOL_GW:(g + 1) * POOL_GW]).astype(BF16)
        y = _dot(d, wp_ref[g]) * sc_ref[:, g * POOL_GW:(g + 1) * POOL_GW]
        o_ref[:, g * POOL_GW:(g + 1) * POOL_GW] = y.astype(BF16)


def _pool(xa, layer, w_pool_bf, scale, *, ts=512):
    B, S, _ = xa.shape
    return pl.pallas_call(
        functools.partial(_pool_kernel, ts=ts),
        grid=(B, S // ts),
        in_specs=[
            pl.BlockSpec((None, ts, POOL_W), lambda b, s: (b, s, 0)),
            _layer_spec(w_pool_bf, layer), _layer_spec(scale, layer),
        ],
        out_specs=pl.BlockSpec((None, ts, POOL_W), lambda b, s: (b, s, 0)),
        out_shape=jax.ShapeDtypeStruct((B, S, POOL_W), BF16),
        scratch_shapes=[pltpu.VMEM((ts + POOL_HALO, POOL_W), F32)],
        compiler_params=_params(2),
        name="pool",
    )(xa, w_pool_bf, scale)


def _rglru_kernel(xb_ref, gb_ref, cw_ref, cb_ref, wg_ref, ba_ref, bx_ref, lam_ref, o_ref,
                  ext_ref, a_ref, u_ref, hc_ref, *, ts, lane_chunk):
    s = pl.program_id(1)

    @pl.when(s == 0)
    def _():
        ext_ref[0:CONV_HALO, :] = jnp.zeros((CONV_HALO, LRU_W), F32)
        hc_ref[...] = jnp.zeros((SUBLANES, LRU_W), F32)

    ext_ref[CONV_HALO:CONV_HALO + ts, :] = xb_ref[...]
    ext = ext_ref[...]
    y = cb_ref[...]
    for k in range(CONV_W):
        lag = CONV_W - 1 - k
        shifted = pltpu.roll(ext, lag, 0) if lag else ext
        y = y + shifted[CONV_HALO:, :] * cw_ref[k:k + 1, :]
    ext_ref[0:CONV_HALO, :] = ext[ts:ts + CONV_HALO, :]

    yb = y.astype(BF16)
    for h in range(LRU_HEADS):
        cols = slice(h * LRU_HD, (h + 1) * LRU_HD)
        gates = _dot(yb[:, cols], wg_ref[h])
        r = jax.nn.sigmoid(gates[:, :LRU_HD] + ba_ref[:, cols])
        i = jax.nn.sigmoid(gates[:, LRU_HD:] + bx_ref[:, cols])
        log_a = -LRU_C * r * _softplus(-lam_ref[:, cols])
        a = jnp.exp(log_a)
        one_minus_a2 = -jnp.tanh(log_a) * (a * a + 1.0)
        a_ref[:, cols] = a
        u_ref[:, cols] = jnp.sqrt(one_minus_a2) * (i * y[:, cols])

    row = lax.broadcasted_iota(jnp.int32, (SUBLANES, lane_chunk), 0)
    for c in range(LRU_W // lane_chunk):
        cols = slice(c * lane_chunk, (c + 1) * lane_chunk)

        def group(gi, h_prev, cols=cols):
            off = pl.multiple_of(gi * SUBLANES, SUBLANES)
            a = a_ref[pl.ds(off, SUBLANES), cols]
            u = u_ref[pl.ds(off, SUBLANES), cols]
            for d in (1, 2, 4):
                a_sh = jnp.where(row >= d, pltpu.roll(a, d, 0), 1.0)
                u_sh = jnp.where(row >= d, pltpu.roll(u, d, 0), 0.0)
                u = a * u_sh + u
                a = a * a_sh
            hh = a * h_prev + u
            u_ref[pl.ds(off, SUBLANES), cols] = hh
            return jnp.broadcast_to(hh[SUBLANES - 1:SUBLANES, :], (SUBLANES, lane_chunk))

        hc_ref[:, cols] = lax.fori_loop(0, ts // SUBLANES, group, hc_ref[:, cols], unroll=2)

    o_ref[...] = (u_ref[...] * jax.nn.gelu(gb_ref[...])).astype(BF16)


def _rglru(xb, gb, layer, conv_w, conv_b, wg_bf, b_a, b_x, lam, *, ts=256, lane_chunk=1024):
    B, S, _ = xb.shape
    tile = pl.BlockSpec((None, ts, LRU_W), lambda b, s: (b, s, 0))
    params = (conv_w, conv_b, wg_bf, b_a, b_x, lam)
    return pl.pallas_call(
        functools.partial(_rglru_kernel, ts=ts, lane_chunk=lane_chunk),
        grid=(B, S // ts),
        in_specs=[tile, tile] + [_layer_spec(p, layer) for p in params],
        out_specs=tile,
        out_shape=jax.ShapeDtypeStruct((B, S, LRU_W), BF16),
        scratch_shapes=[
            pltpu.VMEM((ts + CONV_HALO, LRU_W), F32),
            pltpu.VMEM((ts, LRU_W), F32),
            pltpu.VMEM((ts, LRU_W), F32),
            pltpu.VMEM((SUBLANES, LRU_W), F32),
        ],
        compiler_params=_params(2),
        name="rglru",
    )(xb, gb, *params)


def _logf_cumsum_kernel(fl_ref, bf_ref, o_ref, *, seq):
    z = fl_ref[...] + bf_ref[...]
    x = jnp.minimum(z, 0.0) - jnp.log1p(jnp.exp(-jnp.abs(z)))
    lane = lax.broadcasted_iota(jnp.int32, x.shape, 1)
    d = 1
    while d < seq:
        x = x + jnp.where(lane >= d, pltpu.roll(x, d, 1), 0.0)
        d *= 2
    pieces = []
    rest = -x
    for _ in range(N_SPLIT):
        piece = rest.astype(BF16).astype(F32)
        pieces.append(piece)
        rest = rest - piece
    pad = jnp.zeros((BF16_ROWS - N_SPLIT, seq), F32)
    for h in range(FOX_HEADS):
        o_ref[h] = jnp.concatenate([p[h:h + 1] for p in pieces] + [pad], axis=0).astype(BF16)


def _logf_cumsum(fl_t, layer, b_f, *, seq):
    H, T = fl_t.shape
    return pl.pallas_call(
        functools.partial(_logf_cumsum_kernel, seq=seq),
        grid=(T // seq,),
        in_specs=[pl.BlockSpec((H, seq), lambda b: (0, b)), _layer_spec(b_f, layer)],
        out_specs=pl.BlockSpec((H, BF16_ROWS, seq), lambda b: (0, 0, b)),
        out_shape=jax.ShapeDtypeStruct((H, BF16_ROWS, T), BF16),
        compiler_params=_params(1),
        name="logf_cumsum",
    )(fl_t, b_f)


def _max_over_rows(s):
    while s.shape[0] % (2 * SUBLANES) == 0:
        half = s.shape[0] // 2
        s = jnp.maximum(s[:half], s[half:])
    return jnp.max(s, axis=0, keepdims=True)


def _fox_kernel(qt_ref, kt_ref, c_ref, vt_ref, o_ref, m_ref, acc_ref, s_even_ref, s_odd_ref,
                *, tq, tk, strip):
    qi = pl.program_id(2)
    n_strips = tq // strip

    def unit_rows(n_ones, width):
        row = lax.broadcasted_iota(jnp.int32, (BF16_ROWS, width), 0)
        return jnp.where(row < n_ones, 1.0, 0.0).astype(BF16)

    q_aug = [jnp.concatenate([qt_ref[:, c * strip:(c + 1) * strip], unit_rows(N_SPLIT, strip)],
                             axis=0) for c in range(n_strips)]

    def keys(off, width):
        return jnp.concatenate([kt_ref[:, pl.ds(off, width)], c_ref[:, pl.ds(off, width)]],
                               axis=0)

    def scores(k_aug, c):
        return _dot_tn(k_aug, q_aug[c])

    def update(off, width, c, s, first_query=None):
        if first_query is not None:
            key = lax.broadcasted_iota(jnp.int32, (width, strip), 0)
            qry = lax.broadcasted_iota(jnp.int32, (width, strip), 1) + first_query
            s = jnp.where(key <= qry, s, NEG_BIG)
        m = m_ref[c]
        m_new = jnp.maximum(m, _max_over_rows(s))
        alpha = jnp.exp(m - m_new)
        p = jnp.exp(s - m_new).astype(BF16)
        m_ref[c] = m_new
        v_aug = jnp.concatenate([vt_ref[:, pl.ds(off, width)], unit_rows(1, width)], axis=0)
        acc_ref[c] = alpha * acc_ref[c] + _dot(v_aug, p)

    for c in range(n_strips):
        m_ref[c] = jnp.full((1, strip), -jnp.inf, F32)
        acc_ref[c] = jnp.zeros((AUG_H, strip), F32)

    n_bulk = qi * (tq // tk)
    early = range(0, n_strips // 2)
    late = range(n_strips // 2, n_strips)

    def late_update(j, s_ref):
        off = pl.multiple_of(j * tk, tk)
        for c in late:
            update(off, tk, c, s_ref[c - late.start])

    def stage(j, read_ref, write_ref):
        off = pl.multiple_of(j * tk, tk)
        k_aug = keys(off, tk)
        s_early = [scores(k_aug, c) for c in early]
        if read_ref is not None:
            late_update(j - 1, read_ref)
        for c in late:
            write_ref[c - late.start] = scores(k_aug, c)
        for n, c in enumerate(early):
            update(off, tk, c, s_early[n])

    @pl.when(n_bulk > 0)
    def _():
        stage(0, None, s_even_ref)

    def two_stages(jj, carry):
        stage(2 * jj + 1, s_even_ref, s_odd_ref)
        stage(2 * jj + 2, s_odd_ref, s_even_ref)
        return carry

    lax.fori_loop(0, (n_bulk - 1) // 2, two_stages, 0)
    last = n_bulk - 1
    last_is_odd = last % 2 == 1

    @pl.when(jnp.logical_and(n_bulk > 0, last_is_odd))
    def _():
        stage(last, s_even_ref, s_odd_ref)
        late_update(last, s_odd_ref)

    @pl.when(jnp.logical_and(n_bulk > 0, jnp.logical_not(last_is_odd)))
    def _():
        late_update(last, s_even_ref)

    base = pl.multiple_of(qi * tq, tq)
    k_diag = keys(base, tq)
    s_diag = [scores(k_diag[:, :(c + 1) * strip], c) for c in range(n_strips)]
    for c in range(n_strips):
        update(base, (c + 1) * strip, c, s_diag[c], first_query=c * strip)
        acc = acc_ref[c]
        o_ref[:, c * strip:(c + 1) * strip] = (
            acc[:FOX_HD] / acc[FOX_HD:FOX_HD + 1]).astype(BF16)


def _fox(q_t, k_t, c_pieces, v_t, *, batch, seq, tq=1024, tk=1024, strip=256):
    B, S = batch, seq
    nq = S // tq
    n_strips = tq // strip
    whole_seq = pl.BlockSpec((FOX_HD, S), lambda b, h, i: (h, b))
    q_tile = pl.BlockSpec((FOX_HD, tq), lambda b, h, i: (h, b * nq + i))
    return pl.pallas_call(
        functools.partial(_fox_kernel, tq=tq, tk=tk, strip=strip),
        grid=(B, FOX_HEADS, nq),
        in_specs=[
            q_tile, whole_seq,
            pl.BlockSpec((None, BF16_ROWS, S), lambda b, h, i: (h, 0, b)),
            whole_seq,
        ],
        out_specs=q_tile,
        out_shape=jax.ShapeDtypeStruct((FOX_W, B * S), BF16),
        scratch_shapes=[
            pltpu.VMEM((n_strips, 1, strip), F32),
            pltpu.VMEM((n_strips, AUG_H, strip), F32),
            pltpu.VMEM((n_strips // 2, tk, strip), F32),
            pltpu.VMEM((n_strips // 2, tk, strip), F32),
        ],
        compiler_params=_params(3),
        name="fox_attention",
    )(q_t, k_t, c_pieces, v_t)


def _merge_kernel(x_ref, a_ref, b_ref, ct_ref, gate_ref,
                  ua_ref, ub_ref, uc_ref, wo_ref, o_ref):
    ups = (_dot(a_ref[...], ua_ref[...]), _dot(b_ref[...], ub_ref[...]),
           _dot_tn(ct_ref[...], uc_ref[...]))
    merged = None
    for n, up in enumerate(ups):
        term = gate_ref[:, n * D_MODEL:(n + 1) * D_MODEL].astype(F32) * up
        merged = term if merged is None else merged + term
    o_ref[...] = x_ref[...] + _dot(merged.astype(BF16), wo_ref[...])


def _merge(x, ya, yb, yc_t, gate, layer, ua, ub, uc, wo, *, tm=512):
    T = x.shape[0]

    def rows(width):
        return pl.BlockSpec((tm, width), lambda i: (i, 0))

    weights = (ua, ub, uc, wo)
    return pl.pallas_call(
        _merge_kernel,
        grid=(T // tm,),
        in_specs=[
            rows(D_MODEL), rows(POOL_W), rows(LRU_W),
            pl.BlockSpec((FOX_W, tm), lambda i: (0, i)),
            rows(N_BRANCH * D_MODEL),
        ] + [_layer_spec(w, layer) for w in weights],
        out_specs=rows(D_MODEL),
        out_shape=jax.ShapeDtypeStruct((T, D_MODEL), F32),
        compiler_params=_params(1),
        name="merge",
    )(x, ya, yb, yc_t, gate, *weights)


def _memkv_kernel(m_ref, g_ref, w_ref, k_ref, v_ref):
    h = _rms(m_ref[...], g_ref[...]).astype(BF16)
    k_ref[...] = _dot(h, w_ref[:, :D_MODEL]).astype(BF16)
    v_ref[...] = _dot(h, w_ref[:, D_MODEL:]).astype(BF16)


def _memkv(mem, layer, g, w_kv_bf):
    B = mem.shape[0]
    tile = pl.BlockSpec((None, N_MEM, D_MODEL), lambda b: (b, 0, 0))
    return pl.pallas_call(
        _memkv_kernel,
        grid=(B,),
        in_specs=[tile, _layer_spec(g, layer), _layer_spec(w_kv_bf, layer)],
        out_specs=(tile, tile),
        out_shape=(jax.ShapeDtypeStruct((B, N_MEM, D_MODEL), BF16),) * 2,
        compiler_params=_params(1),
        name="mem_kv",
    )(mem, g, w_kv_bf)


def _cross_kernel(x_ref, g_ref, wq_ref, k_ref, v_ref, wo_ref, o_ref):
    x = x_ref[...]
    h = _rms(x, g_ref[...]).astype(BF16)
    q = (_dot(h, wq_ref[...]) * (X_HD ** -0.5)).astype(BF16)
    heads = []
    for hd in range(X_HEADS):
        cols = slice(hd * X_HD, (hd + 1) * X_HD)
        s = _dot_nt(q[:, cols], k_ref[:, cols])
        e = jnp.exp(s - jnp.max(s, axis=-1, keepdims=True))
        p = e / jnp.sum(e, axis=-1, keepdims=True)
        heads.append(_dot(p.astype(BF16), v_ref[:, cols]).astype(BF16))
    o = jnp.concatenate(heads, axis=-1)
    o_ref[...] = x + _dot(o, wo_ref[...])


def _cross(x, layer, g, wq, k, v, wo, *, tm=512):
    B, S, _ = x.shape
    tile = pl.BlockSpec((None, tm, D_MODEL), lambda b, s: (b, s, 0))
    kv = pl.BlockSpec((None, N_MEM, D_MODEL), lambda b, s: (b, 0, 0))
    return pl.pallas_call(
        _cross_kernel,
        grid=(B, S // tm),
        in_specs=[tile, _layer_spec(g, layer), _layer_spec(wq, layer), kv, kv,
                  _layer_spec(wo, layer)],
        out_specs=tile,
        out_shape=jax.ShapeDtypeStruct((B, S, D_MODEL), F32),
        compiler_params=_params(2),
        name="cross_attention",
    )(x, g, wq, k, v, wo)


def kernel(x, mem, g_ffn1, w_ffn1_in, w_ffn1_out, g_mix, w_in, b_f, b_gate, w_pool, pool_scale,
           w_up_a, conv_w, conv_b, w_rg_a, b_rg_a, w_rg_x, b_rg_x, lru_lambda, w_up_b, w_up_c, w_o,
           g_cross, g_mem, w_xq, w_xkv, w_xo, g_ffn2, w_ffn2_in, w_ffn2_out, g_final):
    B, S, D = x.shape
    T = B * S
    depth = g_ffn1.shape[0]

    def vec(p):
        return p.reshape(p.shape[0], 1, p.shape[1])

    def bf(w):
        return w.astype(BF16)

    g_ffn1, g_mix, g_cross, g_mem, g_ffn2 = map(vec, (g_ffn1, g_mix, g_cross, g_mem, g_ffn2))
    b_gate, pool_scale, conv_b, b_rg_a, b_rg_x, lru_lambda = map(
        vec, (b_gate, pool_scale, conv_b, b_rg_a, b_rg_x, lru_lambda))
    g_fin = g_final.reshape(1, 1, D)
    b_f_col = b_f.reshape(depth, FOX_HEADS, 1)
    w_ffn1_in, w_ffn1_out, w_ffn2_in, w_ffn2_out = map(
        bf, (w_ffn1_in, w_ffn1_out, w_ffn2_in, w_ffn2_out))
    w_pool, w_up_a, w_up_b, w_up_c, w_o, w_xq, w_xkv, w_xo = map(
        bf, (w_pool, w_up_a, w_up_b, w_up_c, w_o, w_xq, w_xkv, w_xo))
    w_main = bf(jnp.concatenate([w_in[:, :, IN_XA:IN_Q], w_in[:, :, IN_GL:]], axis=2))
    w_qkv_t = bf(jnp.swapaxes(w_in[:, :, IN_Q:IN_FL], 1, 2))
    w_fl_t = bf(jnp.swapaxes(w_in[:, :, IN_FL:IN_GL], 1, 2))
    w_gates = bf(jnp.concatenate([w_rg_a, w_rg_x], axis=-1))

    xt = x.reshape(T, D)
    for l in range(depth):
        xt = _ffn(xt, l, g_ffn1, w_ffn1_in, w_ffn1_out, g_fin, final_norm=False)

        xa, xb, gb, gate, q_t, k_t, v_t, fl_t = _inproj(xt, l, g_mix, w_main, w_qkv_t, w_fl_t,
                                                        b_gate)
        ya = _pool(xa.reshape(B, S, POOL_W), l, w_pool, pool_scale)
        yb = _rglru(xb.reshape(B, S, LRU_W), gb.reshape(B, S, LRU_W), l, conv_w, conv_b,
                    w_gates, b_rg_a, b_rg_x, lru_lambda)
        c_pieces = _logf_cumsum(fl_t, l, b_f_col, seq=S)
        yc_t = _fox(q_t, k_t, c_pieces, v_t, batch=B, seq=S)
        xt = _merge(xt, ya.reshape(T, POOL_W), yb.reshape(T, LRU_W), yc_t, gate, l,
                    w_up_a, w_up_b, w_up_c, w_o)

        mk, mv = _memkv(mem, l, g_mem, w_xkv)
        xt = _cross(xt.reshape(B, S, D), l, g_cross, w_xq, mk, mv, w_xo).reshape(T, D)

        xt = _ffn(xt, l, g_ffn2, w_ffn2_in, w_ffn2_out, g_fin, final_norm=(l == depth - 1))
    return xt.reshape(B, S, D)
```

```python
import functools

import jax
import jax.numpy as jnp
from jax import lax
from jax.experimental import pallas as pl
from jax.experimental.pallas import tpu as pltpu

F32 = jnp.float32
BF16 = jnp.bfloat16

D_MODEL = 1024
N_MEM = 256
POOL_WINDOWS = (2, 4, 8, 16)
POOL_GROUPS = 4
POOL_W = 512
POOL_GW = 128
POOL_HALO = 16
LRU_W = 1024
LRU_HEADS = 8
LRU_HD = 128
CONV_W = 4
CONV_HALO = 8
LRU_C = 8.0
FOX_HEADS = 8
FOX_HD = 64
FOX_W = 512
X_HEADS = 4
X_HD = 256
D_FF = 2816
N_BRANCH = 3
EPS = 1e-6
SUBLANES = 8
BF16_ROWS = 16
VMEM_LIMIT = 56 * 1024 * 1024
NEG_BIG = -1e30

IN_XA = 0
IN_XB = IN_XA + POOL_W
IN_GB = IN_XB + LRU_W
IN_Q = IN_GB + LRU_W
IN_K = IN_Q + FOX_W
IN_V = IN_K + FOX_W
IN_FL = IN_V + FOX_W
IN_GL = IN_FL + FOX_HEADS
OFF_XA = 0
OFF_XB = OFF_XA + POOL_W
OFF_GB = OFF_XB + LRU_W
OFF_GL = OFF_GB + LRU_W
IN_MAIN = OFF_GL + N_BRANCH * D_MODEL
N_SPLIT = 3
AUG_H = FOX_HD + BF16_ROWS


def _params(n_grid_dims):
    return pltpu.CompilerParams(
        dimension_semantics=("arbitrary",) * n_grid_dims,
        vmem_limit_bytes=VMEM_LIMIT,
    )


def _layer_spec(arr, layer):
    shape = arr.shape[1:]
    zeros = (0,) * len(shape)
    return pl.BlockSpec((None,) + shape, lambda *_: (layer,) + zeros,
                        pipeline_mode=pl.Buffered(1))


def _rms(x, g):
    ms = jnp.mean(x * x, axis=-1, keepdims=True)
    return x * lax.rsqrt(ms + EPS) * g


def _softplus(z):
    return jnp.maximum(z, 0.0) + jnp.log1p(jnp.exp(-jnp.abs(z)))


def _dot(a, b):
    return jnp.dot(a, b, preferred_element_type=F32)


def _dot_tn(a, b):
    return lax.dot_general(a, b, (((0,), (0,)), ((), ())), preferred_element_type=F32)


def _dot_nt(a, b):
    return lax.dot_general(a, b, (((1,), (1,)), ((), ())), preferred_element_type=F32)


def _ffn_kernel(x_ref, g_ref, wa_ref, wb_ref, wo_ref, gf_ref, o_ref, *, n_chunks, final_norm):
    x = x_ref[...]
    h = _rms(x, g_ref[...]).astype(BF16)
    fc = D_FF // n_chunks
    acc = None
    for c in range(n_chunks):
        a = _dot(h, wa_ref[:, c * fc:(c + 1) * fc])
        b = _dot(h, wb_ref[:, c * fc:(c + 1) * fc])
        act = (a * jax.nn.sigmoid(a) * b).astype(BF16)
        part = _dot(act, wo_ref[c * fc:(c + 1) * fc, :])
        acc = part if acc is None else acc + part
    y = x + 0.5 * acc
    if final_norm:
        y = _rms(y, gf_ref[...])
    o_ref[...] = y


def _ffn(x, layer, g, w_in_bf, w_out_bf, g_final, *, final_norm, tm=1024, n_chunks=2):
    T = x.shape[0]

    def half(j):
        return pl.BlockSpec((None, D_MODEL, D_FF), lambda i: (layer, 0, j),
                            pipeline_mode=pl.Buffered(1))

    return pl.pallas_call(
        functools.partial(_ffn_kernel, n_chunks=n_chunks, final_norm=final_norm),
        grid=(T // tm,),
        in_specs=[
            pl.BlockSpec((tm, D_MODEL), lambda i: (i, 0)),
            _layer_spec(g, layer), half(0), half(1), _layer_spec(w_out_bf, layer),
            _layer_spec(g_final, 0),
        ],
        out_specs=pl.BlockSpec((tm, D_MODEL), lambda i: (i, 0)),
        out_shape=jax.ShapeDtypeStruct((T, D_MODEL), F32),
        compiler_params=_params(1),
        name="ffn",
    )(x, g, w_in_bf, w_in_bf, w_out_bf, g_final)


def _inproj_kernel(x_ref, g_ref, w_ref, wqkv_ref, wfl_ref, bg_ref, cw_ref, cb_ref,
                   xa_ref, conv_ref, gelu_ref, gate_ref, qt_ref, kt_ref, vt_ref, fl_ref,
                   ext_ref, *, tm, tiles_per_seq):
    @pl.when(pl.program_id(0) % tiles_per_seq == 0)
    def _():
        ext_ref[0:CONV_HALO, :] = jnp.zeros((CONV_HALO, LRU_W), F32)

    h = _rms(x_ref[...], g_ref[...]).astype(BF16)

    def seg(lo, width):
        return _dot(h, w_ref[:, lo:lo + width])

    ext_ref[CONV_HALO:CONV_HALO + tm, :] = seg(OFF_XB, LRU_W)
    ext = ext_ref[...]
    y = cb_ref[...]
    for k in range(CONV_W):
        lag = CONV_W - 1 - k
        shifted = pltpu.roll(ext, lag, 0) if lag else ext
        y = y + shifted[CONV_HALO:, :] * cw_ref[k:k + 1, :]
    ext_ref[0:CONV_HALO, :] = ext[tm:tm + CONV_HALO, :]
    conv_ref[...] = y
    gelu_ref[...] = jax.nn.gelu(seg(OFF_GB, LRU_W))
    gate_ref[...] = jax.nn.sigmoid(seg(OFF_GL, N_BRANCH * D_MODEL) + bg_ref[...]).astype(BF16)
    xa_ref[...] = seg(OFF_XA, POOL_W)

    qkv_t = _dot_nt(wqkv_ref[...], h)
    qt_ref[...] = (qkv_t[:FOX_W] * (FOX_HD ** -0.5)).astype(BF16)
    kt_ref[...] = qkv_t[FOX_W:2 * FOX_W].astype(BF16)
    vt_ref[...] = qkv_t[2 * FOX_W:].astype(BF16)
    fl_ref[...] = _dot_nt(wfl_ref[...], h)


def _inproj(x, layer, g, w_main_bf, w_qkv_t_bf, w_fl_t_bf, b_gate, conv_w, conv_b,
            *, seq, tm=256):
    T = x.shape[0]
    params = (g, w_main_bf, w_qkv_t_bf, w_fl_t_bf, b_gate, conv_w, conv_b)

    def rows(width):
        return pl.BlockSpec((tm, width), lambda i: (i, 0))

    def cols(height):
        return pl.BlockSpec((height, tm), lambda i: (0, i))

    out_shape = (
        jax.ShapeDtypeStruct((T, POOL_W), F32),
        jax.ShapeDtypeStruct((T, LRU_W), F32),
        jax.ShapeDtypeStruct((T, LRU_W), F32),
        jax.ShapeDtypeStruct((T, N_BRANCH * D_MODEL), BF16),
        jax.ShapeDtypeStruct((FOX_W, T), BF16),
        jax.ShapeDtypeStruct((FOX_W, T), BF16),
        jax.ShapeDtypeStruct((FOX_W, T), BF16),
        jax.ShapeDtypeStruct((FOX_HEADS, T), F32),
    )
    out_specs = (
        rows(POOL_W), rows(LRU_W), rows(LRU_W), rows(N_BRANCH * D_MODEL),
        cols(FOX_W), cols(FOX_W), cols(FOX_W), cols(FOX_HEADS),
    )
    return pl.pallas_call(
        functools.partial(_inproj_kernel, tm=tm, tiles_per_seq=seq // tm),
        grid=(T // tm,),
        in_specs=[rows(D_MODEL)] + [_layer_spec(p, layer) for p in params],
        out_specs=out_specs,
        out_shape=out_shape,
        scratch_shapes=[pltpu.VMEM((tm + CONV_HALO, LRU_W), F32)],
        compiler_params=_params(1),
        name="inproj",
    )(x, *params)


def _pool_kernel(xa_ref, wp_ref, sc_ref, o_ref, ext_ref, *, ts):
    s = pl.program_id(1)

    @pl.when(s == 0)
    def _():
        ext_ref[0:POOL_HALO, :] = jnp.zeros((POOL_HALO, POOL_W), F32)

    x = xa_ref[...]
    ext_ref[POOL_HALO:POOL_HALO + ts, :] = x
    e = ext_ref[...]
    ext_ref[0:POOL_HALO, :] = e[ts:ts + POOL_HALO, :]

    sums = []
    cur = e
    shift = 1
    for g in range(POOL_GROUPS):
        cur = cur + pltpu.roll(cur, shift, 0)
        sums.append(cur[POOL_HALO:, 0:POOL_GW])
        if g + 1 < POOL_GROUPS:
            cur = cur[:, POOL_GW:]
        shift *= 2

    pos = (s * ts + 1 + lax.broadcasted_iota(jnp.int32, (ts, 1), 0)).astype(F32)
    for g, w in enumerate(POOL_WINDOWS):
        inv_cnt = 1.0 / jnp.minimum(pos, float(w))
        mean = sums[g] * inv_cnt
        d = (mean - x[:, g * POOL_GW:(g + 1) * POOL_GW]).astype(BF16)
        y = _dot(d, wp_ref[g]) * sc_ref[:, g * POOL_GW:(g + 1) * POOL_GW]
        o_ref[:, g * POOL_GW:(g + 1) * POOL_GW] = y.astype(BF16)


def _pool(xa, layer, w_pool_bf, scale, *, ts=512):
    B, S, _ = xa.shape
    return pl.pallas_call(
        functools.partial(_pool_kernel, ts=ts),
        grid=(B, S // ts),
        in_specs=[
            pl.BlockSpec((None, ts, POOL_W), lambda b, s: (b, s, 0)),
            _layer_spec(w_pool_bf, layer), _layer_spec(scale, layer),
        ],
        out_specs=pl.BlockSpec((None, ts, POOL_W), lambda b, s: (b, s, 0)),
        out_shape=jax.ShapeDtypeStruct((B, S, POOL_W), BF16),
        scratch_shapes=[pltpu.VMEM((ts + POOL_HALO, POOL_W), F32)],
        compiler_params=_params(2),
        name="pool",
    )(xa, w_pool_bf, scale)


def _rglru_kernel(y_ref, gelu_ref, wg_ref, ba_ref, bx_ref, lam_ref, o_ref,
                  a_ref, u_ref, hc_ref, *, ts, lane_chunk):
    @pl.when(pl.program_id(1) == 0)
    def _():
        hc_ref[...] = jnp.zeros((SUBLANES, LRU_W), F32)

    y = y_ref[...]
    yb = y.astype(BF16)
    for h in range(LRU_HEADS):
        cols = slice(h * LRU_HD, (h + 1) * LRU_HD)
        gates = _dot(yb[:, cols], wg_ref[h])
        r = jax.nn.sigmoid(gates[:, :LRU_HD] + ba_ref[:, cols])
        i = jax.nn.sigmoid(gates[:, LRU_HD:] + bx_ref[:, cols])
        log_a = -LRU_C * r * _softplus(-lam_ref[:, cols])
        a = jnp.exp(log_a)
        one_minus_a2 = -jnp.tanh(log_a) * (a * a + 1.0)
        a_ref[:, cols] = a
        u_ref[:, cols] = jnp.sqrt(one_minus_a2) * (i * y[:, cols])

    row = lax.broadcasted_iota(jnp.int32, (SUBLANES, lane_chunk), 0)
    for c in range(LRU_W // lane_chunk):
        cols = slice(c * lane_chunk, (c + 1) * lane_chunk)

        def group(gi, h_prev, cols=cols):
            off = pl.multiple_of(gi * SUBLANES, SUBLANES)
            a = a_ref[pl.ds(off, SUBLANES), cols]
            u = u_ref[pl.ds(off, SUBLANES), cols]
            for d in (1, 2, 4):
                a_sh = jnp.where(row >= d, pltpu.roll(a, d, 0), 1.0)
                u_sh = jnp.where(row >= d, pltpu.roll(u, d, 0), 0.0)
                u = a * u_sh + u
                a = a * a_sh
            hh = a * h_prev + u
            u_ref[pl.ds(off, SUBLANES), cols] = hh
            return jnp.broadcast_to(hh[SUBLANES - 1:SUBLANES, :], (SUBLANES, lane_chunk))

        hc_ref[:, cols] = lax.fori_loop(0, ts // SUBLANES, group, hc_ref[:, cols], unroll=2)

    o_ref[...] = (u_ref[...] * gelu_ref[...]).astype(BF16)


def _rglru(y, gelu_gb, layer, wg_bf, b_a, b_x, lam, *, ts=256, lane_chunk=1024):
    B, S, _ = y.shape
    tile = pl.BlockSpec((None, ts, LRU_W), lambda b, s: (b, s, 0))
    params = (wg_bf, b_a, b_x, lam)
    return pl.pallas_call(
        functools.partial(_rglru_kernel, ts=ts, lane_chunk=lane_chunk),
        grid=(B, S // ts),
        in_specs=[tile, tile] + [_layer_spec(p, layer) for p in params],
        out_specs=tile,
        out_shape=jax.ShapeDtypeStruct((B, S, LRU_W), BF16),
        scratch_shapes=[
            pltpu.VMEM((ts, LRU_W), F32),
            pltpu.VMEM((ts, LRU_W), F32),
            pltpu.VMEM((SUBLANES, LRU_W), F32),
        ],
        compiler_params=_params(2),
        name="rglru",
    )(y, gelu_gb, *params)


def _logf_cumsum_kernel(fl_ref, bf_ref, o_ref, *, seq):
    z = fl_ref[...] + bf_ref[...]
    x = jnp.minimum(z, 0.0) - jnp.log1p(jnp.exp(-jnp.abs(z)))
    lane = lax.broadcasted_iota(jnp.int32, x.shape, 1)
    d = 1
    while d < seq:
        x = x + jnp.where(lane >= d, pltpu.roll(x, d, 1), 0.0)
        d *= 2
    pieces = []
    rest = -x
    for _ in range(N_SPLIT):
        piece = rest.astype(BF16).astype(F32)
        pieces.append(piece)
        rest = rest - piece
    pad = jnp.zeros((BF16_ROWS - N_SPLIT, seq), F32)
    for h in range(FOX_HEADS):
        o_ref[h] = jnp.concatenate([p[h:h + 1] for p in pieces] + [pad], axis=0).astype(BF16)


def _logf_cumsum(fl_t, layer, b_f, *, seq):
    H, T = fl_t.shape
    return pl.pallas_call(
        functools.partial(_logf_cumsum_kernel, seq=seq),
        grid=(T // seq,),
        in_specs=[pl.BlockSpec((H, seq), lambda b: (0, b)), _layer_spec(b_f, layer)],
        out_specs=pl.BlockSpec((H, BF16_ROWS, seq), lambda b: (0, 0, b)),
        out_shape=jax.ShapeDtypeStruct((H, BF16_ROWS, T), BF16),
        compiler_params=_params(1),
        name="logf_cumsum",
    )(fl_t, b_f)


def _max_over_rows(s):
    while s.shape[0] % (2 * SUBLANES) == 0:
        half = s.shape[0] // 2
        s = jnp.maximum(s[:half], s[half:])
    return jnp.max(s, axis=0, keepdims=True)


def _fox_kernel(qt_ref, kt_ref, c_ref, vt_ref, o_ref, m_ref, acc_ref, s_even_ref, s_odd_ref,
                *, tq, tk, strip):
    qi = pl.program_id(2)
    n_strips = tq // strip

    def unit_rows(n_ones, width):
        row = lax.broadcasted_iota(jnp.int32, (BF16_ROWS, width), 0)
        return jnp.where(row < n_ones, 1.0, 0.0).astype(BF16)

    q_aug = [jnp.concatenate([qt_ref[:, c * strip:(c + 1) * strip], unit_rows(N_SPLIT, strip)],
                             axis=0) for c in range(n_strips)]

    def keys(off, width):
        return jnp.concatenate([kt_ref[:, pl.ds(off, width)], c_ref[:, pl.ds(off, width)]],
                               axis=0)

    def scores(k_aug, c):
        return _dot_tn(k_aug, q_aug[c])

    def update(off, width, c, s, first_query=None):
        if first_query is not None:
            key = lax.broadcasted_iota(jnp.int32, (width, strip), 0)
            qry = lax.broadcasted_iota(jnp.int32, (width, strip), 1) + first_query
            s = jnp.where(key <= qry, s, NEG_BIG)
        m = m_ref[c]
        m_new = jnp.maximum(m, _max_over_rows(s))
        alpha = jnp.exp(m - m_new)
        p = jnp.exp(s - m_new).astype(BF16)
        m_ref[c] = m_new
        v_aug = jnp.concatenate([vt_ref[:, pl.ds(off, width)], unit_rows(1, width)], axis=0)
        acc_ref[c] = alpha * acc_ref[c] + _dot(v_aug, p)

    for c in range(n_strips):
        m_ref[c] = jnp.full((1, strip), -jnp.inf, F32)
        acc_ref[c] = jnp.zeros((AUG_H, strip), F32)

    n_bulk = qi * (tq // tk)
    early = range(0, n_strips // 2)
    late = range(n_strips // 2, n_strips)

    def late_update(j, s_ref):
        off = pl.multiple_of(j * tk, tk)
        for c in late:
            update(off, tk, c, s_ref[c - late.start])

    def stage(j, read_ref, write_ref):
        off = pl.multiple_of(j * tk, tk)
        k_aug = keys(off, tk)
        s_early = [scores(k_aug, c) for c in early]
        if read_ref is not None:
            late_update(j - 1, read_ref)
        for c in late:
            write_ref[c - late.start] = scores(k_aug, c)
        for n, c in enumerate(early):
            update(off, tk, c, s_early[n])

    @pl.when(n_bulk > 0)
    def _():
        stage(0, None, s_even_ref)

    def two_stages(jj, carry):
        stage(2 * jj + 1, s_even_ref, s_odd_ref)
        stage(2 * jj + 2, s_odd_ref, s_even_ref)
        return carry

    lax.fori_loop(0, (n_bulk - 1) // 2, two_stages, 0)
    last = n_bulk - 1
    last_is_odd = last % 2 == 1

    def tail(pending_ref):
        base = pl.multiple_of(qi * tq, tq)
        k_diag = keys(base, tq)
        s_diag = [scores(k_diag[:, :(c + 1) * strip], c) for c in range(n_strips)]
        if pending_ref is not None:
            late_update(last, pending_ref)
        for c in range(n_strips):
            update(base, (c + 1) * strip, c, s_diag[c], first_query=c * strip)
            acc = acc_ref[c]
            o_ref[:, c * strip:(c + 1) * strip] = (
                acc[:FOX_HD] / acc[FOX_HD:FOX_HD + 1]).astype(BF16)

    @pl.when(n_bulk == 0)
    def _():
        tail(None)

    @pl.when(jnp.logical_and(n_bulk > 0, last_is_odd))
    def _():
        stage(last, s_even_ref, s_odd_ref)
        tail(s_odd_ref)

    @pl.when(jnp.logical_and(n_bulk > 0, jnp.logical_not(last_is_odd)))
    def _():
        tail(s_even_ref)


def _fox(q_t, k_t, c_pieces, v_t, *, batch, seq, tq=1024, tk=1024, strip=256):
    B, S = batch, seq
    nq = S // tq
    n_strips = tq // strip
    whole_seq = pl.BlockSpec((FOX_HD, S), lambda b, h, i: (h, b))
    q_tile = pl.BlockSpec((FOX_HD, tq), lambda b, h, i: (h, b * nq + i))
    return pl.pallas_call(
        functools.partial(_fox_kernel, tq=tq, tk=tk, strip=strip),
        grid=(B, FOX_HEADS, nq),
        in_specs=[
            q_tile, whole_seq,
            pl.BlockSpec((None, BF16_ROWS, S), lambda b, h, i: (h, 0, b)),
            whole_seq,
        ],
        out_specs=q_tile,
        out_shape=jax.ShapeDtypeStruct((FOX_W, B * S), BF16),
        scratch_shapes=[
            pltpu.VMEM((n_strips, 1, strip), F32),
            pltpu.VMEM((n_strips, AUG_H, strip), F32),
            pltpu.VMEM((n_strips // 2, tk, strip), F32),
            pltpu.VMEM((n_strips // 2, tk, strip), F32),
        ],
        compiler_params=_params(3),
        name="fox_attention",
    )(q_t, k_t, c_pieces, v_t)


def _merge_kernel(x_ref, a_ref, b_ref, ct_ref, gate_ref,
                  ua_ref, ub_ref, uc_ref, wo_ref, o_ref):
    ups = (_dot(a_ref[...], ua_ref[...]), _dot(b_ref[...], ub_ref[...]),
           _dot_tn(ct_ref[...], uc_ref[...]))
    merged = None
    for n, up in enumerate(ups):
        term = gate_ref[:, n * D_MODEL:(n + 1) * D_MODEL].astype(F32) * up
        merged = term if merged is None else merged + term
    o_ref[...] = x_ref[...] + _dot(merged.astype(BF16), wo_ref[...])


def _merge(x, ya, yb, yc_t, gate, layer, ua, ub, uc, wo, *, tm=512):
    T = x.shape[0]

    def rows(width):
        return pl.BlockSpec((tm, width), lambda i: (i, 0))

    weights = (ua, ub, uc, wo)
    return pl.pallas_call(
        _merge_kernel,
        grid=(T // tm,),
        in_specs=[
            rows(D_MODEL), rows(POOL_W), rows(LRU_W),
            pl.BlockSpec((FOX_W, tm), lambda i: (0, i)),
            rows(N_BRANCH * D_MODEL),
        ] + [_layer_spec(w, layer) for w in weights],
        out_specs=rows(D_MODEL),
        out_shape=jax.ShapeDtypeStruct((T, D_MODEL), F32),
        compiler_params=_params(1),
        name="merge",
    )(x, ya, yb, yc_t, gate, *weights)


def _memkv_kernel(m_ref, g_ref, w_ref, k_ref, v_ref):
    h = _rms(m_ref[...], g_ref[...]).astype(BF16)
    k_ref[...] = _dot(h, w_ref[:, :D_MODEL]).astype(BF16)
    v_ref[...] = _dot(h, w_ref[:, D_MODEL:]).astype(BF16)


def _memkv(mem, layer, g, w_kv_bf):
    B = mem.shape[0]
    tile = pl.BlockSpec((None, N_MEM, D_MODEL), lambda b: (b, 0, 0))
    return pl.pallas_call(
        _memkv_kernel,
        grid=(B,),
        in_specs=[tile, _layer_spec(g, layer), _layer_spec(w_kv_bf, layer)],
        out_specs=(tile, tile),
        out_shape=(jax.ShapeDtypeStruct((B, N_MEM, D_MODEL), BF16),) * 2,
        compiler_params=_params(1),
        name="mem_kv",
    )(mem, g, w_kv_bf)


def _cross_kernel(x_ref, g_ref, wq_ref, k_ref, v_ref, wo_ref, o_ref):
    x = x_ref[...]
    h = _rms(x, g_ref[...]).astype(BF16)
    q = (_dot(h, wq_ref[...]) * (X_HD ** -0.5)).astype(BF16)
    heads = []
    for hd in range(X_HEADS):
        cols = slice(hd * X_HD, (hd + 1) * X_HD)
        s = _dot_nt(q[:, cols], k_ref[:, cols])
        e = jnp.exp(s - jnp.max(s, axis=-1, keepdims=True))
        p = e / jnp.sum(e, axis=-1, keepdims=True)
        heads.append(_dot(p.astype(BF16), v_ref[:, cols]).astype(BF16))
    o = jnp.concatenate(heads, axis=-1)
    o_ref[...] = x + _dot(o, wo_ref[...])


def _cross(x, layer, g, wq, k, v, wo, *, tm=512):
    B, S, _ = x.shape
    tile = pl.BlockSpec((None, tm, D_MODEL), lambda b, s: (b, s, 0))
    kv = pl.BlockSpec((None, N_MEM, D_MODEL), lambda b, s: (b, 0, 0))
    return pl.pallas_call(
        _cross_kernel,
        grid=(B, S // tm),
        in_specs=[tile, _layer_spec(g, layer), _layer_spec(wq, layer), kv, kv,
                  _layer_spec(wo, layer)],
        out_specs=tile,
        out_shape=jax.ShapeDtypeStruct((B, S, D_MODEL), F32),
        compiler_params=_params(2),
        name="cross_attention",
    )(x, g, wq, k, v, wo)


def kernel(x, mem, g_ffn1, w_ffn1_in, w_ffn1_out, g_mix, w_in, b_f, b_gate, w_pool, pool_scale,
           w_up_a, conv_w, conv_b, w_rg_a, b_rg_a, w_rg_x, b_rg_x, lru_lambda, w_up_b, w_up_c, w_o,
           g_cross, g_mem, w_xq, w_xkv, w_xo, g_ffn2, w_ffn2_in, w_ffn2_out, g_final):
    B, S, D = x.shape
    T = B * S
    depth = g_ffn1.shape[0]

    def vec(p):
        return p.reshape(p.shape[0], 1, p.shape[1])

    def bf(w):
        return w.astype(BF16)

    g_ffn1, g_mix, g_cross, g_mem, g_ffn2 = map(vec, (g_ffn1, g_mix, g_cross, g_mem, g_ffn2))
    b_gate, pool_scale, conv_b, b_rg_a, b_rg_x, lru_lambda = map(
        vec, (b_gate, pool_scale, conv_b, b_rg_a, b_rg_x, lru_lambda))
    g_fin = g_final.reshape(1, 1, D)
    b_f_col = b_f.reshape(depth, FOX_HEADS, 1)
    w_ffn1_in, w_ffn1_out, w_ffn2_in, w_ffn2_out = map(
        bf, (w_ffn1_in, w_ffn1_out, w_ffn2_in, w_ffn2_out))
    w_pool, w_up_a, w_up_b, w_up_c, w_o, w_xq, w_xkv, w_xo = map(
        bf, (w_pool, w_up_a, w_up_b, w_up_c, w_o, w_xq, w_xkv, w_xo))
    w_main = bf(jnp.concatenate([w_in[:, :, IN_XA:IN_Q], w_in[:, :, IN_GL:]], axis=2))
    w_qkv_t = bf(jnp.swapaxes(w_in[:, :, IN_Q:IN_FL], 1, 2))
    w_fl_t = bf(jnp.swapaxes(w_in[:, :, IN_FL:IN_GL], 1, 2))
    w_gates = bf(jnp.concatenate([w_rg_a, w_rg_x], axis=-1))

    xt = x.reshape(T, D)
    for l in range(depth):
        xt = _ffn(xt, l, g_ffn1, w_ffn1_in, w_ffn1_out, g_fin, final_norm=False)

        xa, xb_conv, gb_gelu, gate, q_t, k_t, v_t, fl_t = _inproj(
            xt, l, g_mix, w_main, w_qkv_t, w_fl_t, b_gate, conv_w, conv_b, seq=S)
        ya = _pool(xa.reshape(B, S, POOL_W), l, w_pool, pool_scale)
        yb = _rglru(xb_conv.reshape(B, S, LRU_W), gb_gelu.reshape(B, S, LRU_W), l,
                    w_gates, b_rg_a, b_rg_x, lru_lambda)
        c_pieces = _logf_cumsum(fl_t, l, b_f_col, seq=S)
        yc_t = _fox(q_t, k_t, c_pieces, v_t, batch=B, seq=S)
        xt = _merge(xt, ya.reshape(T, POOL_W), yb.reshape(T, LRU_W), yc_t, gate, l,
                    w_up_a, w_up_b, w_up_c, w_o)

        mk, mv = _memkv(mem, l, g_mem, w_xkv)
        xt = _cross(xt.reshape(B, S, D), l, g_cross, w_xq, mk, mv, w_xo).reshape(T, D)

        xt = _ffn(xt, l, g_ffn2, w_ffn2_in, w_ffn2_out, g_fin, final_norm=(l == depth - 1))
    return xt.reshape(B, S, D)
```

```python
import functools

import jax
import jax.numpy as jnp
from jax import lax
from jax.experimental import pallas as pl
from jax.experimental.pallas import tpu as pltpu

F32 = jnp.float32
BF16 = jnp.bfloat16

D_MODEL = 1024
N_MEM = 256
POOL_WINDOWS = (2, 4, 8, 16)
POOL_GROUPS = 4
POOL_W = 512
POOL_GW = 128
POOL_HALO = 16
LRU_W = 1024
LRU_HEADS = 8
LRU_HD = 128
CONV_W = 4
CONV_HALO = 8
LRU_C = 8.0
FOX_HEADS = 8
FOX_HD = 64
FOX_W = 512
X_HEADS = 4
X_HD = 256
D_FF = 2816
N_BRANCH = 3
EPS = 1e-6
SUBLANES = 8
BF16_ROWS = 16
VMEM_LIMIT = 56 * 1024 * 1024
NEG_BIG = -1e30

IN_XA = 0
IN_XB = IN_XA + POOL_W
IN_GB = IN_XB + LRU_W
IN_Q = IN_GB + LRU_W
IN_K = IN_Q + FOX_W
IN_V = IN_K + FOX_W
IN_FL = IN_V + FOX_W
IN_GL = IN_FL + FOX_HEADS
OFF_XA = 0
OFF_XB = OFF_XA + POOL_W
OFF_GB = OFF_XB + LRU_W
OFF_GL = OFF_GB + LRU_W
IN_MAIN = OFF_GL + N_BRANCH * D_MODEL
N_SPLIT = 3
AUG_H = FOX_HD + BF16_ROWS


def _params(n_grid_dims):
    return pltpu.CompilerParams(
        dimension_semantics=("arbitrary",) * n_grid_dims,
        vmem_limit_bytes=VMEM_LIMIT,
    )


def _layer_spec(arr, layer):
    shape = arr.shape[1:]
    zeros = (0,) * len(shape)
    return pl.BlockSpec((None,) + shape, lambda *_: (layer,) + zeros,
                        pipeline_mode=pl.Buffered(1))


def _rms(x, g):
    ms = jnp.mean(x * x, axis=-1, keepdims=True)
    return x * lax.rsqrt(ms + EPS) * g


def _softplus(z):
    return jnp.maximum(z, 0.0) + jnp.log1p(jnp.exp(-jnp.abs(z)))


def _dot(a, b):
    return jnp.dot(a, b, preferred_element_type=F32)


def _dot_tn(a, b):
    return lax.dot_general(a, b, (((0,), (0,)), ((), ())), preferred_element_type=F32)


def _dot_nt(a, b):
    return lax.dot_general(a, b, (((1,), (1,)), ((), ())), preferred_element_type=F32)


def _ffn_kernel(x_ref, g_ref, wa_ref, wb_ref, wo_ref, gf_ref, o_ref, *, n_chunks, final_norm):
    x = x_ref[...]
    h = _rms(x, g_ref[...]).astype(BF16)
    fc = D_FF // n_chunks
    acc = None
    for c in range(n_chunks):
        a = _dot(h, wa_ref[:, c * fc:(c + 1) * fc])
        b = _dot(h, wb_ref[:, c * fc:(c + 1) * fc])
        act = (a * jax.nn.sigmoid(a) * b).astype(BF16)
        part = _dot(act, wo_ref[c * fc:(c + 1) * fc, :])
        acc = part if acc is None else acc + part
    y = x + 0.5 * acc
    if final_norm:
        y = _rms(y, gf_ref[...])
    o_ref[...] = y


def _ffn(x, layer, g, w_in_bf, w_out_bf, g_final, *, final_norm, tm=1024, n_chunks=11):
    T = x.shape[0]

    def half(j):
        return pl.BlockSpec((None, D_MODEL, D_FF), lambda i: (layer, 0, j),
                            pipeline_mode=pl.Buffered(1))

    return pl.pallas_call(
        functools.partial(_ffn_kernel, n_chunks=n_chunks, final_norm=final_norm),
        grid=(T // tm,),
        in_specs=[
            pl.BlockSpec((tm, D_MODEL), lambda i: (i, 0)),
            _layer_spec(g, layer), half(0), half(1), _layer_spec(w_out_bf, layer),
            _layer_spec(g_final, 0),
        ],
        out_specs=pl.BlockSpec((tm, D_MODEL), lambda i: (i, 0)),
        out_shape=jax.ShapeDtypeStruct((T, D_MODEL), F32),
        compiler_params=_params(1),
        name="ffn",
    )(x, g, w_in_bf, w_in_bf, w_out_bf, g_final)


def _inproj_kernel(x_ref, g_ref, w_ref, wqkv_ref, wfl_ref, bg_ref, cw_ref, cb_ref,
                   xa_ref, conv_ref, gelu_ref, gate_ref, qt_ref, kt_ref, vt_ref, fl_ref,
                   ext_ref, *, tm, tiles_per_seq):
    @pl.when(pl.program_id(0) % tiles_per_seq == 0)
    def _():
        ext_ref[0:CONV_HALO, :] = jnp.zeros((CONV_HALO, LRU_W), F32)

    h = _rms(x_ref[...], g_ref[...]).astype(BF16)

    def seg(lo, width):
        return _dot(h, w_ref[:, lo:lo + width])

    ext_ref[CONV_HALO:CONV_HALO + tm, :] = seg(OFF_XB, LRU_W)
    ext = ext_ref[...]
    y = cb_ref[...]
    for k in range(CONV_W):
        lag = CONV_W - 1 - k
        shifted = pltpu.roll(ext, lag, 0) if lag else ext
        y = y + shifted[CONV_HALO:, :] * cw_ref[k:k + 1, :]
    ext_ref[0:CONV_HALO, :] = ext[tm:tm + CONV_HALO, :]
    conv_ref[...] = y
    gelu_ref[...] = jax.nn.gelu(seg(OFF_GB, LRU_W))
    gate_ref[...] = jax.nn.sigmoid(seg(OFF_GL, N_BRANCH * D_MODEL) + bg_ref[...]).astype(BF16)
    xa_ref[...] = seg(OFF_XA, POOL_W)

    qkv_t = _dot_nt(wqkv_ref[...], h)
    qt_ref[...] = (qkv_t[:FOX_W] * (FOX_HD ** -0.5)).astype(BF16)
    kt_ref[...] = qkv_t[FOX_W:2 * FOX_W].astype(BF16)
    vt_ref[...] = qkv_t[2 * FOX_W:].astype(BF16)
    fl_ref[...] = _dot_nt(wfl_ref[...], h)


def _inproj(x, layer, g, w_main_bf, w_qkv_t_bf, w_fl_t_bf, b_gate, conv_w, conv_b,
            *, seq, tm=256):
    T = x.shape[0]
    params = (g, w_main_bf, w_qkv_t_bf, w_fl_t_bf, b_gate, conv_w, conv_b)

    def rows(width):
        return pl.BlockSpec((tm, width), lambda i: (i, 0))

    def cols(height):
        return pl.BlockSpec((height, tm), lambda i: (0, i))

    out_shape = (
        jax.ShapeDtypeStruct((T, POOL_W), F32),
        jax.ShapeDtypeStruct((T, LRU_W), F32),
        jax.ShapeDtypeStruct((T, LRU_W), F32),
        jax.ShapeDtypeStruct((T, N_BRANCH * D_MODEL), BF16),
        jax.ShapeDtypeStruct((FOX_W, T), BF16),
        jax.ShapeDtypeStruct((FOX_W, T), BF16),
        jax.ShapeDtypeStruct((FOX_W, T), BF16),
        jax.ShapeDtypeStruct((FOX_HEADS, T), F32),
    )
    out_specs = (
        rows(POOL_W), rows(LRU_W), rows(LRU_W), rows(N_BRANCH * D_MODEL),
        cols(FOX_W), cols(FOX_W), cols(FOX_W), cols(FOX_HEADS),
    )
    return pl.pallas_call(
        functools.partial(_inproj_kernel, tm=tm, tiles_per_seq=seq // tm),
        grid=(T // tm,),
        in_specs=[rows(D_MODEL)] + [_layer_spec(p, layer) for p in params],
        out_specs=out_specs,
        out_shape=out_shape,
        scratch_shapes=[pltpu.VMEM((tm + CONV_HALO, LRU_W), F32)],
        compiler_params=_params(1),
        name="inproj",
    )(x, *params)


def _pool_kernel(xa_ref, wp_ref, sc_ref, o_ref, ext_ref, *, ts):
    s = pl.program_id(1)

    @pl.when(s == 0)
    def _():
        ext_ref[0:POOL_HALO, :] = jnp.zeros((POOL_HALO, POOL_W), F32)

    x = xa_ref[...]
    ext_ref[POOL_HALO:POOL_HALO + ts, :] = x
    e = ext_ref[...]
    ext_ref[0:POOL_HALO, :] = e[ts:ts + POOL_HALO, :]

    sums = []
    cur = e
    shift = 1
    for g in range(POOL_GROUPS):
        cur = cur + pltpu.roll(cur, shift, 0)
        sums.append(cur[POOL_HALO:, 0:POOL_GW])
        if g + 1 < POOL_GROUPS:
            cur = cur[:, POOL_GW:]
        shift *= 2

    pos = (s * ts + 1 + lax.broadcasted_iota(jnp.int32, (ts, 1), 0)).astype(F32)
    for g, w in enumerate(POOL_WINDOWS):
        inv_cnt = 1.0 / jnp.minimum(pos, float(w))
        mean = sums[g] * inv_cnt
        d = (mean - x[:, g * POOL_GW:(g + 1) * POOL_GW]).astype(BF16)
        y = _dot(d, wp_ref[g]) * sc_ref[:, g * POOL_GW:(g + 1) * POOL_GW]
        o_ref[:, g * POOL_GW:(g + 1) * POOL_GW] = y.astype(BF16)


def _pool(xa, layer, w_pool_bf, scale, *, ts=1024):
    B, S, _ = xa.shape
    return pl.pallas_call(
        functools.partial(_pool_kernel, ts=ts),
        grid=(B, S // ts),
        in_specs=[
            pl.BlockSpec((None, ts, POOL_W), lambda b, s: (b, s, 0)),
            _layer_spec(w_pool_bf, layer), _layer_spec(scale, layer),
        ],
        out_specs=pl.BlockSpec((None, ts, POOL_W), lambda b, s: (b, s, 0)),
        out_shape=jax.ShapeDtypeStruct((B, S, POOL_W), BF16),
        scratch_shapes=[pltpu.VMEM((ts + POOL_HALO, POOL_W), F32)],
        compiler_params=_params(2),
        name="pool",
    )(xa, w_pool_bf, scale)


def _rglru_kernel(y_ref, gelu_ref, wg_ref, ba_ref, bx_ref, lam_ref, o_ref,
                  a_ref, u_ref, hc_ref, *, ts, lane_chunk):
    @pl.when(pl.program_id(1) == 0)
    def _():
        hc_ref[...] = jnp.zeros((SUBLANES, LRU_W), F32)

    y = y_ref[...]
    yb = y.astype(BF16)
    for h in range(LRU_HEADS):
        cols = slice(h * LRU_HD, (h + 1) * LRU_HD)
        gates = _dot(yb[:, cols], wg_ref[h])
        r = jax.nn.sigmoid(gates[:, :LRU_HD] + ba_ref[:, cols])
        i = jax.nn.sigmoid(gates[:, LRU_HD:] + bx_ref[:, cols])
        log_a = -LRU_C * r * _softplus(-lam_ref[:, cols])
        a = jnp.exp(log_a)
        one_minus_a2 = -jnp.tanh(log_a) * (a * a + 1.0)
        a_ref[:, cols] = a
        u_ref[:, cols] = jnp.sqrt(one_minus_a2) * (i * y[:, cols])

    row = lax.broadcasted_iota(jnp.int32, (SUBLANES, lane_chunk), 0)
    for c in range(LRU_W // lane_chunk):
        cols = slice(c * lane_chunk, (c + 1) * lane_chunk)

        def group(gi, h_prev, cols=cols):
            off = pl.multiple_of(gi * SUBLANES, SUBLANES)
            a = a_ref[pl.ds(off, SUBLANES), cols]
            u = u_ref[pl.ds(off, SUBLANES), cols]
            for d in (1, 2, 4):
                a_sh = jnp.where(row >= d, pltpu.roll(a, d, 0), 1.0)
                u_sh = jnp.where(row >= d, pltpu.roll(u, d, 0), 0.0)
                u = a * u_sh + u
                a = a * a_sh
            hh = a * h_prev + u
            u_ref[pl.ds(off, SUBLANES), cols] = hh
            return jnp.broadcast_to(hh[SUBLANES - 1:SUBLANES, :], (SUBLANES, lane_chunk))

        hc_ref[:, cols] = lax.fori_loop(0, ts // SUBLANES, group, hc_ref[:, cols], unroll=2)

    o_ref[...] = (u_ref[...] * gelu_ref[...]).astype(BF16)


def _rglru(y, gelu_gb, layer, wg_bf, b_a, b_x, lam, *, ts=512, lane_chunk=1024):
    B, S, _ = y.shape
    tile = pl.BlockSpec((None, ts, LRU_W), lambda b, s: (b, s, 0))
    params = (wg_bf, b_a, b_x, lam)
    return pl.pallas_call(
        functools.partial(_rglru_kernel, ts=ts, lane_chunk=lane_chunk),
        grid=(B, S // ts),
        in_specs=[tile, tile] + [_layer_spec(p, layer) for p in params],
        out_specs=tile,
        out_shape=jax.ShapeDtypeStruct((B, S, LRU_W), BF16),
        scratch_shapes=[
            pltpu.VMEM((ts, LRU_W), F32),
            pltpu.VMEM((ts, LRU_W), F32),
            pltpu.VMEM((SUBLANES, LRU_W), F32),
        ],
        compiler_params=_params(2),
        name="rglru",
    )(y, gelu_gb, *params)


def _logf_cumsum_kernel(fl_ref, bf_ref, o_ref, *, seq):
    z = fl_ref[...] + bf_ref[...]
    x = jnp.minimum(z, 0.0) - jnp.log1p(jnp.exp(-jnp.abs(z)))
    lane = lax.broadcasted_iota(jnp.int32, x.shape, 1)
    d = 1
    while d < seq:
        x = x + jnp.where(lane >= d, pltpu.roll(x, d, 1), 0.0)
        d *= 2
    pieces = []
    rest = -x
    for _ in range(N_SPLIT):
        piece = rest.astype(BF16).astype(F32)
        pieces.append(piece)
        rest = rest - piece
    pad = jnp.zeros((BF16_ROWS - N_SPLIT, seq), F32)
    for h in range(FOX_HEADS):
        o_ref[h] = jnp.concatenate([p[h:h + 1] for p in pieces] + [pad], axis=0).astype(BF16)


def _logf_cumsum(fl_t, layer, b_f, *, seq):
    H, T = fl_t.shape
    return pl.pallas_call(
        functools.partial(_logf_cumsum_kernel, seq=seq),
        grid=(T // seq,),
        in_specs=[pl.BlockSpec((H, seq), lambda b: (0, b)), _layer_spec(b_f, layer)],
        out_specs=pl.BlockSpec((H, BF16_ROWS, seq), lambda b: (0, 0, b)),
        out_shape=jax.ShapeDtypeStruct((H, BF16_ROWS, T), BF16),
        compiler_params=_params(1),
        name="logf_cumsum",
    )(fl_t, b_f)


def _max_over_rows(s):
    while s.shape[0] % (2 * SUBLANES) == 0:
        half = s.shape[0] // 2
        s = jnp.maximum(s[:half], s[half:])
    return jnp.max(s, axis=0, keepdims=True)


def _fox_kernel(qt_ref, kt_ref, c_ref, vt_ref, o_ref, m_ref, acc_ref, s_even_ref, s_odd_ref,
                *, tq, tk, strip):
    qi = pl.program_id(2)
    n_strips = tq // strip

    def unit_rows(n_ones, width):
        row = lax.broadcasted_iota(jnp.int32, (BF16_ROWS, width), 0)
        return jnp.where(row < n_ones, 1.0, 0.0).astype(BF16)

    q_aug = [jnp.concatenate([qt_ref[:, c * strip:(c + 1) * strip], unit_rows(N_SPLIT, strip)],
                             axis=0) for c in range(n_strips)]

    def keys(off, width):
        return jnp.concatenate([kt_ref[:, pl.ds(off, width)], c_ref[:, pl.ds(off, width)]],
                               axis=0)

    def scores(k_aug, c):
        return _dot_tn(k_aug, q_aug[c])

    def update(off, width, c, s, first_query=None):
        if first_query is not None:
            key = lax.broadcasted_iota(jnp.int32, (width, strip), 0)
            qry = lax.broadcasted_iota(jnp.int32, (width, strip), 1) + first_query
            s = jnp.where(key <= qry, s, NEG_BIG)
        m = m_ref[c]
        m_new = jnp.maximum(m, _max_over_rows(s))
        alpha = jnp.exp(m - m_new)
        p = jnp.exp(s - m_new).astype(BF16)
        m_ref[c] = m_new
        v_aug = jnp.concatenate([vt_ref[:, pl.ds(off, width)], unit_rows(1, width)], axis=0)
        acc_ref[c] = alpha * acc_ref[c] + _dot(v_aug, p)

    for c in range(n_strips):
        m_ref[c] = jnp.full((1, strip), -jnp.inf, F32)
        acc_ref[c] = jnp.zeros((AUG_H, strip), F32)

    n_bulk = qi * (tq // tk)
    early = range(0, n_strips // 2)
    late = range(n_strips // 2, n_strips)

    def late_update(j, s_ref):
        off = pl.multiple_of(j * tk, tk)
        for c in late:
            update(off, tk, c, s_ref[c - late.start])

    def stage(j, read_ref, write_ref):
        off = pl.multiple_of(j * tk, tk)
        k_aug = keys(off, tk)
        s_early = [scores(k_aug, c) for c in early]
        if read_ref is not None:
            late_update(j - 1, read_ref)
        for c in late:
            write_ref[c - late.start] = scores(k_aug, c)
        for n, c in enumerate(early):
            update(off, tk, c, s_early[n])

    @pl.when(n_bulk > 0)
    def _():
        stage(0, None, s_even_ref)

    def two_stages(jj, carry):
        stage(2 * jj + 1, s_even_ref, s_odd_ref)
        stage(2 * jj + 2, s_odd_ref, s_even_ref)
        return carry

    lax.fori_loop(0, (n_bulk - 1) // 2, two_stages, 0)
    last = n_bulk - 1
    last_is_odd = last % 2 == 1

    def tail(pending_ref):
        base = pl.multiple_of(qi * tq, tq)
        k_diag = keys(base, tq)
        s_diag = [scores(k_diag[:, :(c + 1) * strip], c) for c in range(n_strips)]
        if pending_ref is not None:
            late_update(last, pending_ref)
        for c in range(n_strips):
            update(base, (c + 1) * strip, c, s_diag[c], first_query=c * strip)
            acc = acc_ref[c]
            o_ref[:, c * strip:(c + 1) * strip] = (
                acc[:FOX_HD] / acc[FOX_HD:FOX_HD + 1]).astype(BF16)

    @pl.when(n_bulk == 0)
    def _():
        tail(None)

    @pl.when(jnp.logical_and(n_bulk > 0, last_is_odd))
    def _():
        stage(last, s_even_ref, s_odd_ref)
        tail(s_odd_ref)

    @pl.when(jnp.logical_and(n_bulk > 0, jnp.logical_not(last_is_odd)))
    def _():
        tail(s_even_ref)


def _fox(q_t, k_t, c_pieces, v_t, *, batch, seq, tq=1024, tk=1024, strip=256):
    B, S = batch, seq
    nq = S // tq
    n_strips = tq // strip
    whole_seq = pl.BlockSpec((FOX_HD, S), lambda b, h, i: (h, b))
    q_tile = pl.BlockSpec((FOX_HD, tq), lambda b, h, i: (h, b * nq + i))
    return pl.pallas_call(
        functools.partial(_fox_kernel, tq=tq, tk=tk, strip=strip),
        grid=(B, FOX_HEADS, nq),
        in_specs=[
            q_tile, whole_seq,
            pl.BlockSpec((None, BF16_ROWS, S), lambda b, h, i: (h, 0, b)),
            whole_seq,
        ],
        out_specs=q_tile,
        out_shape=jax.ShapeDtypeStruct((FOX_W, B * S), BF16),
        scratch_shapes=[
            pltpu.VMEM((n_strips, 1, strip), F32),
            pltpu.VMEM((n_strips, AUG_H, strip), F32),
            pltpu.VMEM((n_strips // 2, tk, strip), F32),
            pltpu.VMEM((n_strips // 2, tk, strip), F32),
        ],
        compiler_params=_params(3),
        name="fox_attention",
    )(q_t, k_t, c_pieces, v_t)


def _merge_kernel(x_ref, a_ref, b_ref, ct_ref, gate_ref,
                  ua_ref, ub_ref, uc_ref, wo_ref, o_ref):
    ups = (_dot(a_ref[...], ua_ref[...]), _dot(b_ref[...], ub_ref[...]),
           _dot_tn(ct_ref[...], uc_ref[...]))
    merged = None
    for n, up in enumerate(ups):
        term = gate_ref[:, n * D_MODEL:(n + 1) * D_MODEL].astype(F32) * up
        merged = term if merged is None else merged + term
    o_ref[...] = x_ref[...] + _dot(merged.astype(BF16), wo_ref[...])


def _merge(x, ya, yb, yc_t, gate, layer, ua, ub, uc, wo, *, tm=1024):
    T = x.shape[0]

    def rows(width):
        return pl.BlockSpec((tm, width), lambda i: (i, 0))

    weights = (ua, ub, uc, wo)
    return pl.pallas_call(
        _merge_kernel,
        grid=(T // tm,),
        in_specs=[
            rows(D_MODEL), rows(POOL_W), rows(LRU_W),
            pl.BlockSpec((FOX_W, tm), lambda i: (0, i)),
            rows(N_BRANCH * D_MODEL),
        ] + [_layer_spec(w, layer) for w in weights],
        out_specs=rows(D_MODEL),
        out_shape=jax.ShapeDtypeStruct((T, D_MODEL), F32),
        compiler_params=_params(1),
        name="merge",
    )(x, ya, yb, yc_t, gate, *weights)


def _memkv_kernel(m_ref, g_ref, w_ref, k_ref, v_ref):
    h = _rms(m_ref[...], g_ref[...]).astype(BF16)
    k_ref[...] = _dot(h, w_ref[:, :D_MODEL]).astype(BF16)
    v_ref[...] = _dot(h, w_ref[:, D_MODEL:]).astype(BF16)


def _memkv(mem, layer, g, w_kv_bf):
    B = mem.shape[0]
    tile = pl.BlockSpec((None, N_MEM, D_MODEL), lambda b: (b, 0, 0))
    return pl.pallas_call(
        _memkv_kernel,
        grid=(B,),
        in_specs=[tile, _layer_spec(g, layer), _layer_spec(w_kv_bf, layer)],
        out_specs=(tile, tile),
        out_shape=(jax.ShapeDtypeStruct((B, N_MEM, D_MODEL), BF16),) * 2,
        compiler_params=_params(1),
        name="mem_kv",
    )(mem, g, w_kv_bf)


def _cross_kernel(x_ref, g_ref, wq_ref, k_ref, v_ref, wo_ref, o_ref):
    x = x_ref[...]
    h = _rms(x, g_ref[...]).astype(BF16)
    q = (_dot(h, wq_ref[...]) * (X_HD ** -0.5)).astype(BF16)
    heads = []
    for hd in range(X_HEADS):
        cols = slice(hd * X_HD, (hd + 1) * X_HD)
        s = _dot_nt(q[:, cols], k_ref[:, cols])
        e = jnp.exp(s - jnp.max(s, axis=-1, keepdims=True))
        p = e / jnp.sum(e, axis=-1, keepdims=True)
        heads.append(_dot(p.astype(BF16), v_ref[:, cols]).astype(BF16))
    o = jnp.concatenate(heads, axis=-1)
    o_ref[...] = x + _dot(o, wo_ref[...])


def _cross(x, layer, g, wq, k, v, wo, *, tm=1024):
    B, S, _ = x.shape
    tile = pl.BlockSpec((None, tm, D_MODEL), lambda b, s: (b, s, 0))
    kv = pl.BlockSpec((None, N_MEM, D_MODEL), lambda b, s: (b, 0, 0))
    return pl.pallas_call(
        _cross_kernel,
        grid=(B, S // tm),
        in_specs=[tile, _layer_spec(g, layer), _layer_spec(wq, layer), kv, kv,
                  _layer_spec(wo, layer)],
        out_specs=tile,
        out_shape=jax.ShapeDtypeStruct((B, S, D_MODEL), F32),
        compiler_params=_params(2),
        name="cross_attention",
    )(x, g, wq, k, v, wo)


def kernel(x, mem, g_ffn1, w_ffn1_in, w_ffn1_out, g_mix, w_in, b_f, b_gate, w_pool, pool_scale,
           w_up_a, conv_w, conv_b, w_rg_a, b_rg_a, w_rg_x, b_rg_x, lru_lambda, w_up_b, w_up_c, w_o,
           g_cross, g_mem, w_xq, w_xkv, w_xo, g_ffn2, w_ffn2_in, w_ffn2_out, g_final):
    B, S, D = x.shape
    T = B * S
    depth = g_ffn1.shape[0]

    def vec(p):
        return p.reshape(p.shape[0], 1, p.shape[1])

    def bf(w):
        return w.astype(BF16)

    g_ffn1, g_mix, g_cross, g_mem, g_ffn2 = map(vec, (g_ffn1, g_mix, g_cross, g_mem, g_ffn2))
    b_gate, pool_scale, conv_b, b_rg_a, b_rg_x, lru_lambda = map(
        vec, (b_gate, pool_scale, conv_b, b_rg_a, b_rg_x, lru_lambda))
    g_fin = g_final.reshape(1, 1, D)
    b_f_col = b_f.reshape(depth, FOX_HEADS, 1)
    w_ffn1_in, w_ffn1_out, w_ffn2_in, w_ffn2_out = map(
        bf, (w_ffn1_in, w_ffn1_out, w_ffn2_in, w_ffn2_out))
    w_pool, w_up_a, w_up_b, w_up_c, w_o, w_xq, w_xkv, w_xo = map(
        bf, (w_pool, w_up_a, w_up_b, w_up_c, w_o, w_xq, w_xkv, w_xo))
    w_main = bf(jnp.concatenate([w_in[:, :, IN_XA:IN_Q], w_in[:, :, IN_GL:]], axis=2))
    w_qkv_t = bf(jnp.swapaxes(w_in[:, :, IN_Q:IN_FL], 1, 2))
    w_fl_t = bf(jnp.swapaxes(w_in[:, :, IN_FL:IN_GL], 1, 2))
    w_gates = bf(jnp.concatenate([w_rg_a, w_rg_x], axis=-1))

    xt = x.reshape(T, D)
    for l in range(depth):
        xt = _ffn(xt, l, g_ffn1, w_ffn1_in, w_ffn1_out, g_fin, final_norm=False)

        xa, xb_conv, gb_gelu, gate, q_t, k_t, v_t, fl_t = _inproj(
            xt, l, g_mix, w_main, w_qkv_t, w_fl_t, b_gate, conv_w, conv_b, seq=S)
        ya = _pool(xa.reshape(B, S, POOL_W), l, w_pool, pool_scale)
        yb = _rglru(xb_conv.reshape(B, S, LRU_W), gb_gelu.reshape(B, S, LRU_W), l,
                    w_gates, b_rg_a, b_rg_x, lru_lambda)
        c_pieces = _logf_cumsum(fl_t, l, b_f_col, seq=S)
        yc_t = _fox(q_t, k_t, c_pieces, v_t, batch=B, seq=S)
        xt = _merge(xt, ya.reshape(T, POOL_W), yb.reshape(T, LRU_W), yc_t, gate, l,
                    w_up_a, w_up_b, w_up_c, w_o)

        mk, mv = _memkv(mem, l, g_mem, w_xkv)
        xt = _cross(xt.reshape(B, S, D), l, g_cross, w_xq, mk, mv, w_xo).reshape(T, D)

        xt = _ffn(xt, l, g_ffn2, w_ffn2_in, w_ffn2_out, g_fin, final_norm=(l == depth - 1))
    return xt.reshape(B, S, D)
```

```python
import functools

import jax
import jax.numpy as jnp
from jax import lax
from jax.experimental import pallas as pl
from jax.experimental.pallas import tpu as pltpu

F32 = jnp.float32
BF16 = jnp.bfloat16

D_MODEL = 1024
N_MEM = 256
POOL_WINDOWS = (2, 4, 8, 16)
POOL_GROUPS = 4
POOL_W = 512
POOL_GW = 128
POOL_HALO = 16
LRU_W = 1024
LRU_HEADS = 8
LRU_HD = 128
CONV_W = 4
CONV_HALO = 8
LRU_C = 8.0
FOX_HEADS = 8
FOX_HD = 64
FOX_W = 512
X_HEADS = 4
X_HD = 256
D_FF = 2816
N_BRANCH = 3
EPS = 1e-6
SUBLANES = 8
BF16_ROWS = 16
VMEM_LIMIT = 56 * 1024 * 1024
NEG_BIG = -1e30

IN_XA = 0
IN_XB = IN_XA + POOL_W
IN_GB = IN_XB + LRU_W
IN_Q = IN_GB + LRU_W
IN_K = IN_Q + FOX_W
IN_V = IN_K + FOX_W
IN_FL = IN_V + FOX_W
IN_GL = IN_FL + FOX_HEADS
N_SPLIT = 3
AUG_H = FOX_HD + BF16_ROWS


def _params(n_grid_dims):
    return pltpu.CompilerParams(
        dimension_semantics=("arbitrary",) * n_grid_dims,
        vmem_limit_bytes=VMEM_LIMIT,
    )


def _layer_spec(arr, layer):
    shape = arr.shape[1:]
    zeros = (0,) * len(shape)
    return pl.BlockSpec((None,) + shape, lambda *_: (layer,) + zeros,
                        pipeline_mode=pl.Buffered(1))


def _rms(x, g):
    ms = jnp.mean(x * x, axis=-1, keepdims=True)
    return x * lax.rsqrt(ms + EPS) * g


def _softplus(z):
    return jnp.maximum(z, 0.0) + jnp.log1p(jnp.exp(-jnp.abs(z)))


def _dot(a, b):
    return jnp.dot(a, b, preferred_element_type=F32)


def _dot_tn(a, b):
    return lax.dot_general(a, b, (((0,), (0,)), ((), ())), preferred_element_type=F32)


def _dot_nt(a, b):
    return lax.dot_general(a, b, (((1,), (1,)), ((), ())), preferred_element_type=F32)


def _ffn_kernel(x_ref, g_ref, wa_ref, wb_ref, wo_ref, gf_ref, o_ref, *, n_chunks, final_norm):
    x = x_ref[...]
    h = _rms(x, g_ref[...]).astype(BF16)
    fc = D_FF // n_chunks
    acc = None
    for c in range(n_chunks):
        a = _dot(h, wa_ref[:, c * fc:(c + 1) * fc])
        b = _dot(h, wb_ref[:, c * fc:(c + 1) * fc])
        act = (a * jax.nn.sigmoid(a) * b).astype(BF16)
        part = _dot(act, wo_ref[c * fc:(c + 1) * fc, :])
        acc = part if acc is None else acc + part
    y = x + 0.5 * acc
    if final_norm:
        y = _rms(y, gf_ref[...])
    o_ref[...] = y


def _ffn(x, layer, g, w_in_bf, w_out_bf, g_final, *, final_norm, tm=1024, n_chunks=11):
    T = x.shape[0]

    def half(j):
        return pl.BlockSpec((None, D_MODEL, D_FF), lambda i: (layer, 0, j),
                            pipeline_mode=pl.Buffered(1))

    return pl.pallas_call(
        functools.partial(_ffn_kernel, n_chunks=n_chunks, final_norm=final_norm),
        grid=(T // tm,),
        in_specs=[
            pl.BlockSpec((tm, D_MODEL), lambda i: (i, 0)),
            _layer_spec(g, layer), half(0), half(1), _layer_spec(w_out_bf, layer),
            _layer_spec(g_final, 0),
        ],
        out_specs=pl.BlockSpec((tm, D_MODEL), lambda i: (i, 0)),
        out_shape=jax.ShapeDtypeStruct((T, D_MODEL), F32),
        compiler_params=_params(1),
        name="ffn",
    )(x, g, w_in_bf, w_in_bf, w_out_bf, g_final)


def _inproj_kernel(x_ref, g_ref, w_ref, wgl_ref, wfl_ref, bg_ref, cw_ref, cb_ref,
                   xa_ref, conv_ref, gelu_ref, gate_ref, qt_ref, kt_ref, vt_ref, fl_ref,
                   ext_ref, *, tm, tiles_per_seq):
    @pl.when(pl.program_id(0) % tiles_per_seq == 0)
    def _():
        ext_ref[0:CONV_HALO, :] = jnp.zeros((CONV_HALO, LRU_W), F32)

    h = _rms(x_ref[...], g_ref[...]).astype(BF16)

    def seg(lo, width):
        return _dot(h, w_ref[:, lo:lo + width])

    ext_ref[CONV_HALO:CONV_HALO + tm, :] = seg(IN_XB, LRU_W)
    ext = ext_ref[...]
    y = cb_ref[...]
    for k in range(CONV_W):
        lag = CONV_W - 1 - k
        shifted = pltpu.roll(ext, lag, 0) if lag else ext
        y = y + shifted[CONV_HALO:, :] * cw_ref[k:k + 1, :]
    ext_ref[0:CONV_HALO, :] = ext[tm:tm + CONV_HALO, :]
    conv_ref[...] = y
    gelu_ref[...] = jax.nn.gelu(seg(IN_GB, LRU_W))
    gate_ref[...] = jax.nn.sigmoid(_dot(h, wgl_ref[...]) + bg_ref[...]).astype(BF16)
    xa_ref[...] = seg(IN_XA, POOL_W)

    qkv_t = seg(IN_Q, 3 * FOX_W).T
    qt_ref[...] = (qkv_t[:FOX_W] * (FOX_HD ** -0.5)).astype(BF16)
    kt_ref[...] = qkv_t[FOX_W:2 * FOX_W].astype(BF16)
    vt_ref[...] = qkv_t[2 * FOX_W:].astype(BF16)
    fl_ref[...] = _dot_nt(wfl_ref[...], h)


def _inproj(x, layer, g, w_front_bf, w_gl_bf, w_fl_t_bf, b_gate, conv_w, conv_b,
            *, seq, tm=256):
    T = x.shape[0]
    params = (g, w_front_bf, w_gl_bf, w_fl_t_bf, b_gate, conv_w, conv_b)

    def rows(width):
        return pl.BlockSpec((tm, width), lambda i: (i, 0))

    def cols(height):
        return pl.BlockSpec((height, tm), lambda i: (0, i))

    out_shape = (
        jax.ShapeDtypeStruct((T, POOL_W), F32),
        jax.ShapeDtypeStruct((T, LRU_W), F32),
        jax.ShapeDtypeStruct((T, LRU_W), F32),
        jax.ShapeDtypeStruct((T, N_BRANCH * D_MODEL), BF16),
        jax.ShapeDtypeStruct((FOX_W, T), BF16),
        jax.ShapeDtypeStruct((FOX_W, T), BF16),
        jax.ShapeDtypeStruct((FOX_W, T), BF16),
        jax.ShapeDtypeStruct((FOX_HEADS, T), F32),
    )
    out_specs = (
        rows(POOL_W), rows(LRU_W), rows(LRU_W), rows(N_BRANCH * D_MODEL),
        cols(FOX_W), cols(FOX_W), cols(FOX_W), cols(FOX_HEADS),
    )
    return pl.pallas_call(
        functools.partial(_inproj_kernel, tm=tm, tiles_per_seq=seq // tm),
        grid=(T // tm,),
        in_specs=[rows(D_MODEL)] + [_layer_spec(p, layer) for p in params],
        out_specs=out_specs,
        out_shape=out_shape,
        scratch_shapes=[pltpu.VMEM((tm + CONV_HALO, LRU_W), F32)],
        compiler_params=_params(1),
        name="inproj",
    )(x, *params)


def _pool_kernel(xa_ref, wp_ref, sc_ref, o_ref, ext_ref, *, ts):
    s = pl.program_id(1)

    @pl.when(s == 0)
    def _():
        ext_ref[0:POOL_HALO, :] = jnp.zeros((POOL_HALO, POOL_W), F32)

    x = xa_ref[...]
    ext_ref[POOL_HALO:POOL_HALO + ts, :] = x
    e = ext_ref[...]
    ext_ref[0:POOL_HALO, :] = e[ts:ts + POOL_HALO, :]

    sums = []
    cur = e
    shift = 1
    for g in range(POOL_GROUPS):
        cur = cur + pltpu.roll(cur, shift, 0)
        sums.append(cur[POOL_HALO:, 0:POOL_GW])
        if g + 1 < POOL_GROUPS:
            cur = cur[:, POOL_GW:]
        shift *= 2

    pos = (s * ts + 1 + lax.broadcasted_iota(jnp.int32, (ts, 1), 0)).astype(F32)
    for g, w in enumerate(POOL_WINDOWS):
        inv_cnt = 1.0 / jnp.minimum(pos, float(w))
        mean = sums[g] * inv_cnt
        d = (mean - x[:, g * POOL_GW:(g + 1) * POOL_GW]).astype(BF16)
        y = _dot(d, wp_ref[g]) * sc_ref[:, g * POOL_GW:(g + 1) * POOL_GW]
        o_ref[:, g * POOL_GW:(g + 1) * POOL_GW] = y.astype(BF16)


def _pool(xa, layer, w_pool_bf, scale, *, ts=1024):
    B, S, _ = xa.shape
    return pl.pallas_call(
        functools.partial(_pool_kernel, ts=ts),
        grid=(B, S // ts),
        in_specs=[
            pl.BlockSpec((None, ts, POOL_W), lambda b, s: (b, s, 0)),
            _layer_spec(w_pool_bf, layer), _layer_spec(scale, layer),
        ],
        out_specs=pl.BlockSpec((None, ts, POOL_W), lambda b, s: (b, s, 0)),
        out_shape=jax.ShapeDtypeStruct((B, S, POOL_W), BF16),
        scratch_shapes=[pltpu.VMEM((ts + POOL_HALO, POOL_W), F32)],
        compiler_params=_params(2),
        name="pool",
    )(xa, w_pool_bf, scale)


def _rglru_kernel(y_ref, gelu_ref, wg_ref, ba_ref, bx_ref, lam_ref, o_ref,
                  a_ref, u_ref, hc_ref, *, ts, lane_chunk):
    @pl.when(pl.program_id(1) == 0)
    def _():
        hc_ref[...] = jnp.zeros((SUBLANES, LRU_W), F32)

    y = y_ref[...]
    yb = y.astype(BF16)
    for h in range(LRU_HEADS):
        cols = slice(h * LRU_HD, (h + 1) * LRU_HD)
        gates = _dot(yb[:, cols], wg_ref[h])
        r = jax.nn.sigmoid(gates[:, :LRU_HD] + ba_ref[:, cols])
        i = jax.nn.sigmoid(gates[:, LRU_HD:] + bx_ref[:, cols])
        log_a = -LRU_C * r * _softplus(-lam_ref[:, cols])
        a = jnp.exp(log_a)
        one_minus_a2 = -jnp.tanh(log_a) * (a * a + 1.0)
        a_ref[:, cols] = a
        u_ref[:, cols] = jnp.sqrt(one_minus_a2) * (i * y[:, cols])

    row = lax.broadcasted_iota(jnp.int32, (SUBLANES, lane_chunk), 0)
    for c in range(LRU_W // lane_chunk):
        cols = slice(c * lane_chunk, (c + 1) * lane_chunk)

        def group(gi, h_prev, cols=cols):
            off = pl.multiple_of(gi * SUBLANES, SUBLANES)
            a = a_ref[pl.ds(off, SUBLANES), cols]
            u = u_ref[pl.ds(off, SUBLANES), cols]
            for d in (1, 2, 4):
                a_sh = jnp.where(row >= d, pltpu.roll(a, d, 0), 1.0)
                u_sh = jnp.where(row >= d, pltpu.roll(u, d, 0), 0.0)
                u = a * u_sh + u
                a = a * a_sh
            hh = a * h_prev + u
            u_ref[pl.ds(off, SUBLANES), cols] = hh
            return jnp.broadcast_to(hh[SUBLANES - 1:SUBLANES, :], (SUBLANES, lane_chunk))

        hc_ref[:, cols] = lax.fori_loop(0, ts // SUBLANES, group, hc_ref[:, cols], unroll=2)

    o_ref[...] = (u_ref[...] * gelu_ref[...]).astype(BF16)


def _rglru(y, gelu_gb, layer, wg_bf, b_a, b_x, lam, *, ts=512, lane_chunk=1024):
    B, S, _ = y.shape
    tile = pl.BlockSpec((None, ts, LRU_W), lambda b, s: (b, s, 0))
    params = (wg_bf, b_a, b_x, lam)
    return pl.pallas_call(
        functools.partial(_rglru_kernel, ts=ts, lane_chunk=lane_chunk),
        grid=(B, S // ts),
        in_specs=[tile, tile] + [_layer_spec(p, layer) for p in params],
        out_specs=tile,
        out_shape=jax.ShapeDtypeStruct((B, S, LRU_W), BF16),
        scratch_shapes=[
            pltpu.VMEM((ts, LRU_W), F32),
            pltpu.VMEM((ts, LRU_W), F32),
            pltpu.VMEM((SUBLANES, LRU_W), F32),
        ],
        compiler_params=_params(2),
        name="rglru",
    )(y, gelu_gb, *params)


def _logf_cumsum_kernel(fl_ref, bf_ref, o_ref, *, seq):
    z = fl_ref[...] + bf_ref[...]
    x = jnp.minimum(z, 0.0) - jnp.log1p(jnp.exp(-jnp.abs(z)))
    lane = lax.broadcasted_iota(jnp.int32, x.shape, 1)
    d = 1
    while d < seq:
        x = x + jnp.where(lane >= d, pltpu.roll(x, d, 1), 0.0)
        d *= 2
    pieces = []
    rest = -x
    for _ in range(N_SPLIT):
        piece = rest.astype(BF16).astype(F32)
        pieces.append(piece)
        rest = rest - piece
    pad = jnp.zeros((BF16_ROWS - N_SPLIT, seq), F32)
    for h in range(FOX_HEADS):
        o_ref[h] = jnp.concatenate([p[h:h + 1] for p in pieces] + [pad], axis=0).astype(BF16)


def _logf_cumsum(fl_t, layer, b_f, *, seq):
    H, T = fl_t.shape
    return pl.pallas_call(
        functools.partial(_logf_cumsum_kernel, seq=seq),
        grid=(T // seq,),
        in_specs=[pl.BlockSpec((H, seq), lambda b: (0, b)), _layer_spec(b_f, layer)],
        out_specs=pl.BlockSpec((H, BF16_ROWS, seq), lambda b: (0, 0, b)),
        out_shape=jax.ShapeDtypeStruct((H, BF16_ROWS, T), BF16),
        compiler_params=_params(1),
        name="logf_cumsum",
    )(fl_t, b_f)


def _max_over_rows(s):
    while s.shape[0] % (2 * SUBLANES) == 0:
        half = s.shape[0] // 2
        s = jnp.maximum(s[:half], s[half:])
    return jnp.max(s, axis=0, keepdims=True)


def _fox_kernel(qt_ref, kt_ref, c_ref, vt_ref, o_ref, m_ref, acc_ref, s_even_ref, s_odd_ref,
                *, tq, tk, strip):
    qi = pl.program_id(2)
    n_strips = tq // strip

    def unit_rows(n_ones, width):
        row = lax.broadcasted_iota(jnp.int32, (BF16_ROWS, width), 0)
        return jnp.where(row < n_ones, 1.0, 0.0).astype(BF16)

    q_aug = [jnp.concatenate([qt_ref[:, c * strip:(c + 1) * strip], unit_rows(N_SPLIT, strip)],
                             axis=0) for c in range(n_strips)]

    def keys(off, width):
        return jnp.concatenate([kt_ref[:, pl.ds(off, width)], c_ref[:, pl.ds(off, width)]],
                               axis=0)

    def scores(k_aug, c):
        return _dot_tn(k_aug, q_aug[c])

    def update(off, width, c, s, first_query=None):
        if first_query is not None:
            key = lax.broadcasted_iota(jnp.int32, (width, strip), 0)
            qry = lax.broadcasted_iota(jnp.int32, (width, strip), 1) + first_query
            s = jnp.where(key <= qry, s, NEG_BIG)
        m = m_ref[c]
        m_new = jnp.maximum(m, _max_over_rows(s))
        alpha = jnp.exp(m - m_new)
        p = jnp.exp(s - m_new).astype(BF16)
        m_ref[c] = m_new
        v_aug = jnp.concatenate([vt_ref[:, pl.ds(off, width)], unit_rows(1, width)], axis=0)
        acc_ref[c] = alpha * acc_ref[c] + _dot(v_aug, p)

    for c in range(n_strips):
        m_ref[c] = jnp.full((1, strip), -jnp.inf, F32)
        acc_ref[c] = jnp.zeros((AUG_H, strip), F32)

    n_bulk = qi * (tq // tk)
    early = range(0, n_strips // 2)
    late = range(n_strips // 2, n_strips)

    def late_update(j, s_ref):
        off = pl.multiple_of(j * tk, tk)
        for c in late:
            update(off, tk, c, s_ref[c - late.start])

    def stage(j, read_ref, write_ref):
        off = pl.multiple_of(j * tk, tk)
        k_aug = keys(off, tk)
        s_early = [scores(k_aug, c) for c in early]
        if read_ref is not None:
            late_update(j - 1, read_ref)
        for c in late:
            write_ref[c - late.start] = scores(k_aug, c)
        for n, c in enumerate(early):
            update(off, tk, c, s_early[n])

    @pl.when(n_bulk > 0)
    def _():
        stage(0, None, s_even_ref)

    def two_stages(jj, carry):
        stage(2 * jj + 1, s_even_ref, s_odd_ref)
        stage(2 * jj + 2, s_odd_ref, s_even_ref)
        return carry

    lax.fori_loop(0, (n_bulk - 1) // 2, two_stages, 0)
    last = n_bulk - 1
    last_is_odd = last % 2 == 1

    def tail(pending_ref):
        base = pl.multiple_of(qi * tq, tq)
        k_diag = keys(base, tq)
        s_diag = [scores(k_diag[:, :(c + 1) * strip], c) for c in range(n_strips)]
        if pending_ref is not None:
            late_update(last, pending_ref)
        for c in range(n_strips):
            update(base, (c + 1) * strip, c, s_diag[c], first_query=c * strip)
            acc = acc_ref[c]
            o_ref[:, c * strip:(c + 1) * strip] = (
                acc[:FOX_HD] / acc[FOX_HD:FOX_HD + 1]).astype(BF16)

    @pl.when(n_bulk == 0)
    def _():
        tail(None)

    @pl.when(jnp.logical_and(n_bulk > 0, last_is_odd))
    def _():
        stage(last, s_even_ref, s_odd_ref)
        tail(s_odd_ref)

    @pl.when(jnp.logical_and(n_bulk > 0, jnp.logical_not(last_is_odd)))
    def _():
        tail(s_even_ref)


def _fox(q_t, k_t, c_pieces, v_t, *, batch, seq, tq=1024, tk=1024, strip=256):
    B, S = batch, seq
    nq = S // tq
    n_strips = tq // strip
    whole_seq = pl.BlockSpec((FOX_HD, S), lambda b, h, i: (h, b))
    q_tile = pl.BlockSpec((FOX_HD, tq), lambda b, h, i: (h, b * nq + i))
    return pl.pallas_call(
        functools.partial(_fox_kernel, tq=tq, tk=tk, strip=strip),
        grid=(B, FOX_HEADS, nq),
        in_specs=[
            q_tile, whole_seq,
            pl.BlockSpec((None, BF16_ROWS, S), lambda b, h, i: (h, 0, b)),
            whole_seq,
        ],
        out_specs=q_tile,
        out_shape=jax.ShapeDtypeStruct((FOX_W, B * S), BF16),
        scratch_shapes=[
            pltpu.VMEM((n_strips, 1, strip), F32),
            pltpu.VMEM((n_strips, AUG_H, strip), F32),
            pltpu.VMEM((n_strips // 2, tk, strip), F32),
            pltpu.VMEM((n_strips // 2, tk, strip), F32),
        ],
        compiler_params=_params(3),
        name="fox_attention",
    )(q_t, k_t, c_pieces, v_t)


def _merge_cross_kernel(x_ref, a_ref, b_ref, ct_ref, gate_ref, mk_ref, mv_ref,
                        ua_ref, ub_ref, uc_ref, wo_ref, g_ref, wq_ref, wxo_ref, o_ref):
    ups = (_dot(a_ref[...], ua_ref[...]), _dot(b_ref[...], ub_ref[...]),
           _dot_tn(ct_ref[...], uc_ref[...]))
    merged = None
    for n, up in enumerate(ups):
        term = gate_ref[:, n * D_MODEL:(n + 1) * D_MODEL].astype(F32) * up
        merged = term if merged is None else merged + term
    x = x_ref[...] + _dot(merged.astype(BF16), wo_ref[...])

    h = _rms(x, g_ref[...]).astype(BF16)
    q = (_dot(h, wq_ref[...]) * (X_HD ** -0.5)).astype(BF16)
    heads = []
    for hd in range(X_HEADS):
        cols = slice(hd * X_HD, (hd + 1) * X_HD)
        s = _dot_nt(q[:, cols], mk_ref[:, cols])
        e = jnp.exp(s - jnp.max(s, axis=-1, keepdims=True))
        p = e / jnp.sum(e, axis=-1, keepdims=True)
        heads.append(_dot(p.astype(BF16), mv_ref[:, cols]).astype(BF16))
    o_ref[...] = x + _dot(jnp.concatenate(heads, axis=-1), wxo_ref[...])


def _merge_cross(x, ya, yb, yc_t, gate, mk, mv, layer, ua, ub, uc, wo, g_cross, wq, wxo,
                 *, seq, tm=1024):
    T = x.shape[0]
    tiles_per_seq = seq // tm

    def rows(width):
        return pl.BlockSpec((tm, width), lambda i: (i, 0))

    mem_kv = pl.BlockSpec((None, N_MEM, D_MODEL), lambda i: (i // tiles_per_seq, 0, 0))
    params = (ua, ub, uc, wo, g_cross, wq, wxo)
    return pl.pallas_call(
        _merge_cross_kernel,
        grid=(T // tm,),
        in_specs=[
            rows(D_MODEL), rows(POOL_W), rows(LRU_W),
            pl.BlockSpec((FOX_W, tm), lambda i: (0, i)),
            rows(N_BRANCH * D_MODEL), mem_kv, mem_kv,
        ] + [_layer_spec(p, layer) for p in params],
        out_specs=rows(D_MODEL),
        out_shape=jax.ShapeDtypeStruct((T, D_MODEL), F32),
        compiler_params=_params(1),
        name="merge_cross",
    )(x, ya, yb, yc_t, gate, mk, mv, *params)


def _memkv_kernel(m_ref, g_ref, w_ref, k_ref, v_ref):
    h = _rms(m_ref[...], g_ref[...]).astype(BF16)
    k_ref[...] = _dot(h, w_ref[:, :D_MODEL]).astype(BF16)
    v_ref[...] = _dot(h, w_ref[:, D_MODEL:]).astype(BF16)


def _memkv(mem, layer, g, w_kv_bf):
    B = mem.shape[0]
    tile = pl.BlockSpec((None, N_MEM, D_MODEL), lambda b: (b, 0, 0))
    return pl.pallas_call(
        _memkv_kernel,
        grid=(B,),
        in_specs=[tile, _layer_spec(g, layer), _layer_spec(w_kv_bf, layer)],
        out_specs=(tile, tile),
        out_shape=(jax.ShapeDtypeStruct((B, N_MEM, D_MODEL), BF16),) * 2,
        compiler_params=_params(1),
        name="mem_kv",
    )(mem, g, w_kv_bf)


def kernel(x, mem, g_ffn1, w_ffn1_in, w_ffn1_out, g_mix, w_in, b_f, b_gate, w_pool, pool_scale,
           w_up_a, conv_w, conv_b, w_rg_a, b_rg_a, w_rg_x, b_rg_x, lru_lambda, w_up_b, w_up_c, w_o,
           g_cross, g_mem, w_xq, w_xkv, w_xo, g_ffn2, w_ffn2_in, w_ffn2_out, g_final):
    B, S, D = x.shape
    T = B * S
    depth = g_ffn1.shape[0]

    def vec(p):
        return p.reshape(p.shape[0], 1, p.shape[1])

    def bf(w):
        return w.astype(BF16)

    g_ffn1, g_mix, g_cross, g_mem, g_ffn2 = map(vec, (g_ffn1, g_mix, g_cross, g_mem, g_ffn2))
    b_gate, pool_scale, conv_b, b_rg_a, b_rg_x, lru_lambda = map(
        vec, (b_gate, pool_scale, conv_b, b_rg_a, b_rg_x, lru_lambda))
    g_fin = g_final.reshape(1, 1, D)
    b_f_col = b_f.reshape(depth, FOX_HEADS, 1)
    w_ffn1_in, w_ffn1_out, w_ffn2_in, w_ffn2_out = map(
        bf, (w_ffn1_in, w_ffn1_out, w_ffn2_in, w_ffn2_out))
    w_pool, w_up_a, w_up_b, w_up_c, w_o, w_xq, w_xkv, w_xo = map(
        bf, (w_pool, w_up_a, w_up_b, w_up_c, w_o, w_xq, w_xkv, w_xo))
    w_front = bf(w_in[:, :, :IN_FL])
    w_gl = bf(w_in[:, :, IN_GL:])
    w_fl_t = bf(jnp.swapaxes(w_in[:, :, IN_FL:IN_GL], 1, 2))
    w_gates = bf(jnp.concatenate([w_rg_a, w_rg_x], axis=-1))

    xt = x.reshape(T, D)
    for l in range(depth):
        xt = _ffn(xt, l, g_ffn1, w_ffn1_in, w_ffn1_out, g_fin, final_norm=False)

        xa, xb_conv, gb_gelu, gate, q_t, k_t, v_t, fl_t = _inproj(
            xt, l, g_mix, w_front, w_gl, w_fl_t, b_gate, conv_w, conv_b, seq=S)
        ya = _pool(xa.reshape(B, S, POOL_W), l, w_pool, pool_scale)
        yb = _rglru(xb_conv.reshape(B, S, LRU_W), gb_gelu.reshape(B, S, LRU_W), l,
                    w_gates, b_rg_a, b_rg_x, lru_lambda)
        c_pieces = _logf_cumsum(fl_t, l, b_f_col, seq=S)
        yc_t = _fox(q_t, k_t, c_pieces, v_t, batch=B, seq=S)
        mk, mv = _memkv(mem, l, g_mem, w_xkv)
        xt = _merge_cross(xt, ya.reshape(T, POOL_W), yb.reshape(T, LRU_W), yc_t, gate, mk, mv, l,
                          w_up_a, w_up_b, w_up_c, w_o, g_cross, w_xq, w_xo, seq=S)

        xt = _ffn(xt, l, g_ffn2, w_ffn2_in, w_ffn2_out, g_fin, final_norm=(l == depth - 1))
    return xt.reshape(B, S, D)
```

```python
import functools

import jax
import jax.numpy as jnp
from jax import lax
from jax.experimental import pallas as pl
from jax.experimental.pallas import tpu as pltpu

F32 = jnp.float32
BF16 = jnp.bfloat16

D_MODEL = 1024
N_MEM = 256
POOL_WINDOWS = (2, 4, 8, 16)
POOL_GROUPS = 4
POOL_W = 512
POOL_GW = 128
POOL_HALO = 16
LRU_W = 1024
LRU_HEADS = 8
LRU_HD = 128
CONV_W = 4
CONV_HALO = 8
LRU_C = 8.0
FOX_HEADS = 8
FOX_HD = 64
FOX_W = 512
X_HEADS = 4
X_HD = 256
D_FF = 2816
N_BRANCH = 3
EPS = 1e-6
SUBLANES = 8
BF16_ROWS = 16
VMEM_LIMIT = 56 * 1024 * 1024
NEG_BIG = -1e30

IN_XA = 0
IN_XB = IN_XA + POOL_W
IN_GB = IN_XB + LRU_W
IN_Q = IN_GB + LRU_W
IN_K = IN_Q + FOX_W
IN_V = IN_K + FOX_W
IN_FL = IN_V + FOX_W
IN_GL = IN_FL + FOX_HEADS
N_SPLIT = 3
AUG_H = FOX_HD + BF16_ROWS


def _params(n_grid_dims):
    return pltpu.CompilerParams(
        dimension_semantics=("arbitrary",) * n_grid_dims,
        vmem_limit_bytes=VMEM_LIMIT,
    )


def _layer_spec(arr, layer):
    shape = arr.shape[1:]
    zeros = (0,) * len(shape)
    return pl.BlockSpec((None,) + shape, lambda *_: (layer,) + zeros,
                        pipeline_mode=pl.Buffered(1))


def _rms(x, g):
    ms = jnp.mean(x * x, axis=-1, keepdims=True)
    return x * lax.rsqrt(ms + EPS) * g


def _softplus(z):
    return jnp.maximum(z, 0.0) + jnp.log1p(jnp.exp(-jnp.abs(z)))


def _dot(a, b):
    return jnp.dot(a, b, preferred_element_type=F32)


def _dot_tn(a, b):
    return lax.dot_general(a, b, (((0,), (0,)), ((), ())), preferred_element_type=F32)


def _dot_nt(a, b):
    return lax.dot_general(a, b, (((1,), (1,)), ((), ())), preferred_element_type=F32)


def _ffn_kernel(x_ref, g_ref, wa_ref, wb_ref, wo_ref, gf_ref, o_ref, *, n_chunks, final_norm):
    x = x_ref[...]
    h = _rms(x, g_ref[...]).astype(BF16)
    fc = D_FF // n_chunks
    acc = None
    for c in range(n_chunks):
        a = _dot(h, wa_ref[:, c * fc:(c + 1) * fc])
        b = _dot(h, wb_ref[:, c * fc:(c + 1) * fc])
        act = (a * jax.nn.sigmoid(a) * b).astype(BF16)
        part = _dot(act, wo_ref[c * fc:(c + 1) * fc, :])
        acc = part if acc is None else acc + part
    y = x + 0.5 * acc
    if final_norm:
        y = _rms(y, gf_ref[...])
    o_ref[...] = y


def _ffn(x, layer, g, w_in_bf, w_out_bf, g_final, *, final_norm, tm=1024, n_chunks=11):
    T = x.shape[0]

    def half(j):
        return pl.BlockSpec((None, D_MODEL, D_FF), lambda i: (layer, 0, j),
                            pipeline_mode=pl.Buffered(1))

    return pl.pallas_call(
        functools.partial(_ffn_kernel, n_chunks=n_chunks, final_norm=final_norm),
        grid=(T // tm,),
        in_specs=[
            pl.BlockSpec((tm, D_MODEL), lambda i: (i, 0)),
            _layer_spec(g, layer), half(0), half(1), _layer_spec(w_out_bf, layer),
            _layer_spec(g_final, 0),
        ],
        out_specs=pl.BlockSpec((tm, D_MODEL), lambda i: (i, 0)),
        out_shape=jax.ShapeDtypeStruct((T, D_MODEL), F32),
        compiler_params=_params(1),
        name="ffn",
    )(x, g, w_in_bf, w_in_bf, w_out_bf, g_final)


def _inproj_kernel(x_ref, g_ref, w_ref, wgl_ref, wfl_ref, bg_ref, cw_ref, cb_ref,
                   xa_ref, conv_ref, gelu_ref, gate_ref, qt_ref, kt_ref, vt_ref, fl_ref,
                   ext_ref, *, tm, tiles_per_seq):
    @pl.when(pl.program_id(0) % tiles_per_seq == 0)
    def _():
        ext_ref[0:CONV_HALO, :] = jnp.zeros((CONV_HALO, LRU_W), F32)

    h = _rms(x_ref[...], g_ref[...]).astype(BF16)

    def seg(lo, width):
        return _dot(h, w_ref[:, lo:lo + width])

    ext_ref[CONV_HALO:CONV_HALO + tm, :] = seg(IN_XB, LRU_W)
    ext = ext_ref[...]
    y = cb_ref[...]
    for k in range(CONV_W):
        lag = CONV_W - 1 - k
        shifted = pltpu.roll(ext, lag, 0) if lag else ext
        y = y + shifted[CONV_HALO:, :] * cw_ref[k:k + 1, :]
    ext_ref[0:CONV_HALO, :] = ext[tm:tm + CONV_HALO, :]
    conv_ref[...] = y
    gelu_ref[...] = jax.nn.gelu(seg(IN_GB, LRU_W))
    gate_ref[...] = jax.nn.sigmoid(_dot(h, wgl_ref[...]) + bg_ref[...]).astype(BF16)
    xa_ref[...] = seg(IN_XA, POOL_W)

    qkv_t = seg(IN_Q, 3 * FOX_W).T
    qt_ref[...] = (qkv_t[:FOX_W] * (FOX_HD ** -0.5)).astype(BF16)
    kt_ref[...] = qkv_t[FOX_W:2 * FOX_W].astype(BF16)
    vt_ref[...] = qkv_t[2 * FOX_W:].astype(BF16)
    fl_ref[...] = _dot_nt(wfl_ref[...], h)


def _inproj(x, layer, g, w_in_bf, w_gl_bf, w_fl_t_bf, b_gate, conv_w, conv_b,
            *, seq, tm=256):
    T = x.shape[0]
    params = (g, w_in_bf, w_gl_bf, w_fl_t_bf, b_gate, conv_w, conv_b)
    w_front = pl.BlockSpec((None, D_MODEL, IN_FL), lambda i: (layer, 0, 0),
                           pipeline_mode=pl.Buffered(1))
    param_specs = [w_front if p is w_in_bf else _layer_spec(p, layer) for p in params]

    def rows(width):
        return pl.BlockSpec((tm, width), lambda i: (i, 0))

    def cols(height):
        return pl.BlockSpec((height, tm), lambda i: (0, i))

    out_shape = (
        jax.ShapeDtypeStruct((T, POOL_W), F32),
        jax.ShapeDtypeStruct((T, LRU_W), F32),
        jax.ShapeDtypeStruct((T, LRU_W), F32),
        jax.ShapeDtypeStruct((T, N_BRANCH * D_MODEL), BF16),
        jax.ShapeDtypeStruct((FOX_W, T), BF16),
        jax.ShapeDtypeStruct((FOX_W, T), BF16),
        jax.ShapeDtypeStruct((FOX_W, T), BF16),
        jax.ShapeDtypeStruct((FOX_HEADS, T), F32),
    )
    out_specs = (
        rows(POOL_W), rows(LRU_W), rows(LRU_W), rows(N_BRANCH * D_MODEL),
        cols(FOX_W), cols(FOX_W), cols(FOX_W), cols(FOX_HEADS),
    )
    return pl.pallas_call(
        functools.partial(_inproj_kernel, tm=tm, tiles_per_seq=seq // tm),
        grid=(T // tm,),
        in_specs=[rows(D_MODEL)] + param_specs,
        out_specs=out_specs,
        out_shape=out_shape,
        scratch_shapes=[pltpu.VMEM((tm + CONV_HALO, LRU_W), F32)],
        compiler_params=_params(1),
        name="inproj",
    )(x, *params)


def _pool_kernel(xa_ref, wp_ref, sc_ref, o_ref, ext_ref, *, ts):
    s = pl.program_id(1)

    @pl.when(s == 0)
    def _():
        ext_ref[0:POOL_HALO, :] = jnp.zeros((POOL_HALO, POOL_W), F32)

    x = xa_ref[...]
    ext_ref[POOL_HALO:POOL_HALO + ts, :] = x
    e = ext_ref[...]
    ext_ref[0:POOL_HALO, :] = e[ts:ts + POOL_HALO, :]

    sums = []
    cur = e
    shift = 1
    for g in range(POOL_GROUPS):
        cur = cur + pltpu.roll(cur, shift, 0)
        sums.append(cur[POOL_HALO:, 0:POOL_GW])
        if g + 1 < POOL_GROUPS:
            cur = cur[:, POOL_GW:]
        shift *= 2

    pos = (s * ts + 1 + lax.broadcasted_iota(jnp.int32, (ts, 1), 0)).astype(F32)
    for g, w in enumerate(POOL_WINDOWS):
        inv_cnt = 1.0 / jnp.minimum(pos, float(w))
        mean = sums[g] * inv_cnt
        d = (mean - x[:, g * POOL_GW:(g + 1) * POOL_GW]).astype(BF16)
        y = _dot(d, wp_ref[g]) * sc_ref[:, g * POOL_GW:(g + 1) * POOL_GW]
        o_ref[:, g * POOL_GW:(g + 1) * POOL_GW] = y.astype(BF16)


def _pool(xa, layer, w_pool_bf, scale, *, ts=1024):
    B, S, _ = xa.shape
    return pl.pallas_call(
        functools.partial(_pool_kernel, ts=ts),
        grid=(B, S // ts),
        in_specs=[
            pl.BlockSpec((None, ts, POOL_W), lambda b, s: (b, s, 0)),
            _layer_spec(w_pool_bf, layer), _layer_spec(scale, layer),
        ],
        out_specs=pl.BlockSpec((None, ts, POOL_W), lambda b, s: (b, s, 0)),
        out_shape=jax.ShapeDtypeStruct((B, S, POOL_W), BF16),
        scratch_shapes=[pltpu.VMEM((ts + POOL_HALO, POOL_W), F32)],
        compiler_params=_params(2),
        name="pool",
    )(xa, w_pool_bf, scale)


def _rglru_kernel(y_ref, gelu_ref, wg_ref, ba_ref, bx_ref, lam_ref, o_ref,
                  a_ref, u_ref, hc_ref, *, ts, lane_chunk):
    @pl.when(pl.program_id(1) == 0)
    def _():
        hc_ref[...] = jnp.zeros((SUBLANES, LRU_W), F32)

    y = y_ref[...]
    yb = y.astype(BF16)
    for h in range(LRU_HEADS):
        cols = slice(h * LRU_HD, (h + 1) * LRU_HD)
        gates = _dot(yb[:, cols], wg_ref[h])
        r = jax.nn.sigmoid(gates[:, :LRU_HD] + ba_ref[:, cols])
        i = jax.nn.sigmoid(gates[:, LRU_HD:] + bx_ref[:, cols])
        log_a = -LRU_C * r * _softplus(-lam_ref[:, cols])
        a = jnp.exp(log_a)
        one_minus_a2 = -jnp.tanh(log_a) * (a * a + 1.0)
        a_ref[:, cols] = a
        u_ref[:, cols] = jnp.sqrt(one_minus_a2) * (i * y[:, cols])

    row = lax.broadcasted_iota(jnp.int32, (SUBLANES, lane_chunk), 0)
    for c in range(LRU_W // lane_chunk):
        cols = slice(c * lane_chunk, (c + 1) * lane_chunk)

        def group(gi, h_prev, cols=cols):
            off = pl.multiple_of(gi * SUBLANES, SUBLANES)
            a = a_ref[pl.ds(off, SUBLANES), cols]
            u = u_ref[pl.ds(off, SUBLANES), cols]
            for d in (1, 2, 4):
                a_sh = jnp.where(row >= d, pltpu.roll(a, d, 0), 1.0)
                u_sh = jnp.where(row >= d, pltpu.roll(u, d, 0), 0.0)
                u = a * u_sh + u
                a = a * a_sh
            hh = a * h_prev + u
            u_ref[pl.ds(off, SUBLANES), cols] = hh
            return jnp.broadcast_to(hh[SUBLANES - 1:SUBLANES, :], (SUBLANES, lane_chunk))

        hc_ref[:, cols] = lax.fori_loop(0, ts // SUBLANES, group, hc_ref[:, cols], unroll=4)

    o_ref[...] = (u_ref[...] * gelu_ref[...]).astype(BF16)


def _rglru(y, gelu_gb, layer, wg_bf, b_a, b_x, lam, *, ts=512, lane_chunk=1024):
    B, S, _ = y.shape
    tile = pl.BlockSpec((None, ts, LRU_W), lambda b, s: (b, s, 0))
    params = (wg_bf, b_a, b_x, lam)
    return pl.pallas_call(
        functools.partial(_rglru_kernel, ts=ts, lane_chunk=lane_chunk),
        grid=(B, S // ts),
        in_specs=[tile, tile] + [_layer_spec(p, layer) for p in params],
        out_specs=tile,
        out_shape=jax.ShapeDtypeStruct((B, S, LRU_W), BF16),
        scratch_shapes=[
            pltpu.VMEM((ts, LRU_W), F32),
            pltpu.VMEM((ts, LRU_W), F32),
            pltpu.VMEM((SUBLANES, LRU_W), F32),
        ],
        compiler_params=_params(2),
        name="rglru",
    )(y, gelu_gb, *params)


def _logf_cumsum_kernel(fl_ref, bf_ref, o_ref, *, seq):
    z = fl_ref[...] + bf_ref[...]
    x = jnp.minimum(z, 0.0) - jnp.log1p(jnp.exp(-jnp.abs(z)))
    lane = lax.broadcasted_iota(jnp.int32, x.shape, 1)
    d = 1
    while d < seq:
        x = x + jnp.where(lane >= d, pltpu.roll(x, d, 1), 0.0)
        d *= 2
    pieces = []
    rest = -x
    for _ in range(N_SPLIT):
        piece = rest.astype(BF16).astype(F32)
        pieces.append(piece)
        rest = rest - piece
    pad = jnp.zeros((BF16_ROWS - N_SPLIT, seq), F32)
    for h in range(FOX_HEADS):
        o_ref[h] = jnp.concatenate([p[h:h + 1] for p in pieces] + [pad], axis=0).astype(BF16)


def _logf_cumsum(fl_t, layer, b_f, *, seq):
    H, T = fl_t.shape
    return pl.pallas_call(
        functools.partial(_logf_cumsum_kernel, seq=seq),
        grid=(T // seq,),
        in_specs=[pl.BlockSpec((H, seq), lambda b: (0, b)), _layer_spec(b_f, layer)],
        out_specs=pl.BlockSpec((H, BF16_ROWS, seq), lambda b: (0, 0, b)),
        out_shape=jax.ShapeDtypeStruct((H, BF16_ROWS, T), BF16),
        compiler_params=_params(1),
        name="logf_cumsum",
    )(fl_t, b_f)


def _max_over_rows(s):
    while s.shape[0] % (2 * SUBLANES) == 0:
        half = s.shape[0] // 2
        s = jnp.maximum(s[:half], s[half:])
    return jnp.max(s, axis=0, keepdims=True)


def _fox_kernel(qt_ref, kt_ref, c_ref, vt_ref, o_ref, m_ref, acc_ref, s_even_ref, s_odd_ref,
                *, tq, tk, strip):
    qi = pl.program_id(2)
    n_strips = tq // strip

    def unit_rows(n_ones, width):
        row = lax.broadcasted_iota(jnp.int32, (BF16_ROWS, width), 0)
        return jnp.where(row < n_ones, 1.0, 0.0).astype(BF16)

    q_aug = [jnp.concatenate([qt_ref[:, c * strip:(c + 1) * strip], unit_rows(N_SPLIT, strip)],
                             axis=0) for c in range(n_strips)]

    def keys(off, width):
        return jnp.concatenate([kt_ref[:, pl.ds(off, width)], c_ref[:, pl.ds(off, width)]],
                               axis=0)

    def scores(k_aug, c):
        return _dot_tn(k_aug, q_aug[c])

    def update(off, width, c, s, first_query=None):
        if first_query is not None:
            key = lax.broadcasted_iota(jnp.int32, (width, strip), 0)
            qry = lax.broadcasted_iota(jnp.int32, (width, strip), 1) + first_query
            s = jnp.where(key <= qry, s, NEG_BIG)
        m = m_ref[c]
        m_new = jnp.maximum(m, _max_over_rows(s))
        alpha = jnp.exp(m - m_new)
        p = jnp.exp(s - m_new).astype(BF16)
        m_ref[c] = m_new
        v_aug = jnp.concatenate([vt_ref[:, pl.ds(off, width)], unit_rows(1, width)], axis=0)
        acc_ref[c] = alpha * acc_ref[c] + _dot(v_aug, p)

    for c in range(n_strips):
        m_ref[c] = jnp.full((1, strip), -jnp.inf, F32)
        acc_ref[c] = jnp.zeros((AUG_H, strip), F32)

    n_bulk = qi * (tq // tk)
    early = range(0, n_strips // 2)
    late = range(n_strips // 2, n_strips)

    def late_update(j, s_ref):
        off = pl.multiple_of(j * tk, tk)
        for c in late:
            update(off, tk, c, s_ref[c - late.start])

    def stage(j, read_ref, write_ref):
        off = pl.multiple_of(j * tk, tk)
        k_aug = keys(off, tk)
        s_early = [scores(k_aug, c) for c in early]
        if read_ref is not None:
            late_update(j - 1, read_ref)
        for c in late:
            write_ref[c - late.start] = scores(k_aug, c)
        for n, c in enumerate(early):
            update(off, tk, c, s_early[n])

    @pl.when(n_bulk > 0)
    def _():
        stage(0, None, s_even_ref)

    def two_stages(jj, carry):
        stage(2 * jj + 1, s_even_ref, s_odd_ref)
        stage(2 * jj + 2, s_odd_ref, s_even_ref)
        return carry

    lax.fori_loop(0, (n_bulk - 1) // 2, two_stages, 0)
    last = n_bulk - 1
    last_is_odd = last % 2 == 1

    def tail(pending_ref):
        base = pl.multiple_of(qi * tq, tq)
        k_diag = keys(base, tq)
        s_diag = [scores(k_diag[:, :(c + 1) * strip], c) for c in range(n_strips)]
        if pending_ref is not None:
            late_update(last, pending_ref)
        for c in range(n_strips):
            update(base, (c + 1) * strip, c, s_diag[c], first_query=c * strip)
            acc = acc_ref[c]
            o_ref[:, c * strip:(c + 1) * strip] = (
                acc[:FOX_HD] / acc[FOX_HD:FOX_HD + 1]).astype(BF16)

    @pl.when(n_bulk == 0)
    def _():
        tail(None)

    @pl.when(jnp.logical_and(n_bulk > 0, last_is_odd))
    def _():
        stage(last, s_even_ref, s_odd_ref)
        tail(s_odd_ref)

    @pl.when(jnp.logical_and(n_bulk > 0, jnp.logical_not(last_is_odd)))
    def _():
        tail(s_even_ref)


def _fox(q_t, k_t, c_pieces, v_t, *, batch, seq, tq=1024, tk=1024, strip=256):
    B, S = batch, seq
    nq = S // tq
    n_strips = tq // strip
    whole_seq = pl.BlockSpec((FOX_HD, S), lambda b, h, i: (h, b))
    q_tile = pl.BlockSpec((FOX_HD, tq), lambda b, h, i: (h, b * nq + i))
    return pl.pallas_call(
        functools.partial(_fox_kernel, tq=tq, tk=tk, strip=strip),
        grid=(B, FOX_HEADS, nq),
        in_specs=[
            q_tile, whole_seq,
            pl.BlockSpec((None, BF16_ROWS, S), lambda b, h, i: (h, 0, b)),
            whole_seq,
        ],
        out_specs=q_tile,
        out_shape=jax.ShapeDtypeStruct((FOX_W, B * S), BF16),
        scratch_shapes=[
            pltpu.VMEM((n_strips, 1, strip), F32),
            pltpu.VMEM((n_strips, AUG_H, strip), F32),
            pltpu.VMEM((n_strips // 2, tk, strip), F32),
            pltpu.VMEM((n_strips // 2, tk, strip), F32),
        ],
        compiler_params=_params(3),
        name="fox_attention",
    )(q_t, k_t, c_pieces, v_t)


def _merge_cross_kernel(x_ref, a_ref, b_ref, ct_ref, gate_ref, mk_ref, mv_ref,
                        ua_ref, ub_ref, uc_ref, wo_ref, g_ref, wq_ref, wxo_ref, o_ref):
    ups = (_dot(a_ref[...], ua_ref[...]), _dot(b_ref[...], ub_ref[...]),
           _dot_tn(ct_ref[...], uc_ref[...]))
    merged = None
    for n, up in enumerate(ups):
        term = gate_ref[:, n * D_MODEL:(n + 1) * D_MODEL].astype(F32) * up
        merged = term if merged is None else merged + term
    x = x_ref[...] + _dot(merged.astype(BF16), wo_ref[...])

    h = _rms(x, g_ref[...]).astype(BF16)
    q = (_dot(h, wq_ref[...]) * (X_HD ** -0.5)).astype(BF16)
    heads = []
    for hd in range(X_HEADS):
        cols = slice(hd * X_HD, (hd + 1) * X_HD)
        s = _dot_nt(q[:, cols], mk_ref[:, cols])
        e = jnp.exp(s - jnp.max(s, axis=-1, keepdims=True))
        p = e / jnp.sum(e, axis=-1, keepdims=True)
        heads.append(_dot(p.astype(BF16), mv_ref[:, cols]).astype(BF16))
    o_ref[...] = x + _dot(jnp.concatenate(heads, axis=-1), wxo_ref[...])


def _merge_cross(x, ya, yb, yc_t, gate, mk, mv, layer, ua, ub, uc, wo, g_cross, wq, wxo,
                 *, seq, tm=1024):
    T = x.shape[0]
    tiles_per_seq = seq // tm

    def rows(width):
        return pl.BlockSpec((tm, width), lambda i: (i, 0))

    mem_kv = pl.BlockSpec((None, N_MEM, D_MODEL), lambda i: (i // tiles_per_seq, 0, 0))
    params = (ua, ub, uc, wo, g_cross, wq, wxo)
    return pl.pallas_call(
        _merge_cross_kernel,
        grid=(T // tm,),
        in_specs=[
            rows(D_MODEL), rows(POOL_W), rows(LRU_W),
            pl.BlockSpec((FOX_W, tm), lambda i: (0, i)),
            rows(N_BRANCH * D_MODEL), mem_kv, mem_kv,
        ] + [_layer_spec(p, layer) for p in params],
        out_specs=rows(D_MODEL),
        out_shape=jax.ShapeDtypeStruct((T, D_MODEL), F32),
        compiler_params=_params(1),
        name="merge_cross",
    )(x, ya, yb, yc_t, gate, mk, mv, *params)


def _memkv_kernel(m_ref, g_ref, w_ref, k_ref, v_ref):
    h = _rms(m_ref[...], g_ref[...]).astype(BF16)
    k_ref[...] = _dot(h, w_ref[:, :D_MODEL]).astype(BF16)
    v_ref[...] = _dot(h, w_ref[:, D_MODEL:]).astype(BF16)


def _memkv(mem, layer, g, w_kv_bf):
    B = mem.shape[0]
    tile = pl.BlockSpec((None, N_MEM, D_MODEL), lambda b: (b, 0, 0))
    return pl.pallas_call(
        _memkv_kernel,
        grid=(B,),
        in_specs=[tile, _layer_spec(g, layer), _layer_spec(w_kv_bf, layer)],
        out_specs=(tile, tile),
        out_shape=(jax.ShapeDtypeStruct((B, N_MEM, D_MODEL), BF16),) * 2,
        compiler_params=_params(1),
        name="mem_kv",
    )(mem, g, w_kv_bf)


def kernel(x, mem, g_ffn1, w_ffn1_in, w_ffn1_out, g_mix, w_in, b_f, b_gate, w_pool, pool_scale,
           w_up_a, conv_w, conv_b, w_rg_a, b_rg_a, w_rg_x, b_rg_x, lru_lambda, w_up_b, w_up_c, w_o,
           g_cross, g_mem, w_xq, w_xkv, w_xo, g_ffn2, w_ffn2_in, w_ffn2_out, g_final):
    B, S, D = x.shape
    T = B * S
    depth = g_ffn1.shape[0]

    def vec(p):
        return p.reshape(p.shape[0], 1, p.shape[1])

    def bf(w):
        return w.astype(BF16)

    g_ffn1, g_mix, g_cross, g_mem, g_ffn2 = map(vec, (g_ffn1, g_mix, g_cross, g_mem, g_ffn2))
    b_gate, pool_scale, conv_b, b_rg_a, b_rg_x, lru_lambda = map(
        vec, (b_gate, pool_scale, conv_b, b_rg_a, b_rg_x, lru_lambda))
    g_fin = g_final.reshape(1, 1, D)
    b_f_col = b_f.reshape(depth, FOX_HEADS, 1)
    w_ffn1_in, w_ffn1_out, w_ffn2_in, w_ffn2_out = map(
        bf, (w_ffn1_in, w_ffn1_out, w_ffn2_in, w_ffn2_out))
    w_pool, w_up_a, w_up_b, w_up_c, w_o, w_xq, w_xkv, w_xo = map(
        bf, (w_pool, w_up_a, w_up_b, w_up_c, w_o, w_xq, w_xkv, w_xo))
    w_in = bf(w_in)
    w_gl = w_in[:, :, IN_GL:]
    w_fl_t = jnp.swapaxes(w_in[:, :, IN_FL:IN_GL], 1, 2)
    w_gates = bf(jnp.concatenate([w_rg_a, w_rg_x], axis=-1))

    xt = x.reshape(T, D)
    for l in range(depth):
        xt = _ffn(xt, l, g_ffn1, w_ffn1_in, w_ffn1_out, g_fin, final_norm=False)

        xa, xb_conv, gb_gelu, gate, q_t, k_t, v_t, fl_t = _inproj(
            xt, l, g_mix, w_in, w_gl, w_fl_t, b_gate, conv_w, conv_b, seq=S)
        ya = _pool(xa.reshape(B, S, POOL_W), l, w_pool, pool_scale)
        yb = _rglru(xb_conv.reshape(B, S, LRU_W), gb_gelu.reshape(B, S, LRU_W), l,
                    w_gates, b_rg_a, b_rg_x, lru_lambda)
        c_pieces = _logf_cumsum(fl_t, l, b_f_col, seq=S)
        yc_t = _fox(q_t, k_t, c_pieces, v_t, batch=B, seq=S)
        mk, mv = _memkv(mem, l, g_mem, w_xkv)
        xt = _merge_cross(xt, ya.reshape(T, POOL_W), yb.reshape(T, LRU_W), yc_t, gate, mk, mv, l,
                          w_up_a, w_up_b, w_up_c, w_o, g_cross, w_xq, w_xo, seq=S)

        xt = _ffn(xt, l, g_ffn2, w_ffn2_in, w_ffn2_out, g_fin, final_norm=(l == depth - 1))
    return xt.reshape(B, S, D)
```

```python
import functools

import jax
import jax.numpy as jnp
from jax import lax
from jax.experimental import pallas as pl
from jax.experimental.pallas import tpu as pltpu

F32 = jnp.float32
BF16 = jnp.bfloat16

D_MODEL = 1024
N_MEM = 256
POOL_WINDOWS = (2, 4, 8, 16)
POOL_GROUPS = 4
POOL_W = 512
POOL_GW = 128
POOL_HALO = 16
LRU_W = 1024
LRU_HEADS = 8
LRU_HD = 128
CONV_W = 4
CONV_HALO = 8
LRU_C = 8.0
FOX_HEADS = 8
FOX_HD = 64
FOX_W = 512
X_HEADS = 4
X_HD = 256
D_FF = 2816
N_BRANCH = 3
EPS = 1e-6
SUBLANES = 8
BF16_ROWS = 16
VMEM_LIMIT = 56 * 1024 * 1024
NEG_BIG = -1e30

IN_XA = 0
IN_XB = IN_XA + POOL_W
IN_GB = IN_XB + LRU_W
IN_Q = IN_GB + LRU_W
IN_K = IN_Q + FOX_W
IN_V = IN_K + FOX_W
IN_FL = IN_V + FOX_W
IN_GL = IN_FL + FOX_HEADS
N_SPLIT = 3
AUG_H = FOX_HD + BF16_ROWS
N_EARLY = 1


def _params(n_grid_dims):
    return pltpu.CompilerParams(
        dimension_semantics=("arbitrary",) * n_grid_dims,
        vmem_limit_bytes=VMEM_LIMIT,
    )


def _layer_spec(arr, layer):
    shape = arr.shape[1:]
    zeros = (0,) * len(shape)
    return pl.BlockSpec((None,) + shape, lambda *_: (layer,) + zeros,
                        pipeline_mode=pl.Buffered(1))


def _rms(x, g):
    ms = jnp.mean(x * x, axis=-1, keepdims=True)
    return x * lax.rsqrt(ms + EPS) * g


def _softplus(z):
    return jnp.maximum(z, 0.0) + jnp.log1p(jnp.exp(-jnp.abs(z)))


def _dot(a, b):
    return jnp.dot(a, b, preferred_element_type=F32)


def _dot_tn(a, b):
    return lax.dot_general(a, b, (((0,), (0,)), ((), ())), preferred_element_type=F32)


def _dot_nt(a, b):
    return lax.dot_general(a, b, (((1,), (1,)), ((), ())), preferred_element_type=F32)


def _ffn_kernel(x_ref, g_ref, wa_ref, wb_ref, wo_ref, gf_ref, o_ref, *, n_chunks, final_norm):
    x = x_ref[...]
    h = _rms(x, g_ref[...]).astype(BF16)
    fc = D_FF // n_chunks
    acc = None
    for c in range(n_chunks):
        a = _dot(h, wa_ref[:, c * fc:(c + 1) * fc])
        b = _dot(h, wb_ref[:, c * fc:(c + 1) * fc])
        act = (a * jax.nn.sigmoid(a) * b).astype(BF16)
        part = _dot(act, wo_ref[c * fc:(c + 1) * fc, :])
        acc = part if acc is None else acc + part
    y = x + 0.5 * acc
    if final_norm:
        y = _rms(y, gf_ref[...])
    o_ref[...] = y


def _ffn(x, layer, g, w_in_bf, w_out_bf, g_final, *, final_norm, tm=1024, n_chunks=11):
    T = x.shape[0]

    def half(j):
        return pl.BlockSpec((None, D_MODEL, D_FF), lambda i: (layer, 0, j),
                            pipeline_mode=pl.Buffered(1))

    return pl.pallas_call(
        functools.partial(_ffn_kernel, n_chunks=n_chunks, final_norm=final_norm),
        grid=(T // tm,),
        in_specs=[
            pl.BlockSpec((tm, D_MODEL), lambda i: (i, 0)),
            _layer_spec(g, layer), half(0), half(1), _layer_spec(w_out_bf, layer),
            _layer_spec(g_final, 0),
        ],
        out_specs=pl.BlockSpec((tm, D_MODEL), lambda i: (i, 0)),
        out_shape=jax.ShapeDtypeStruct((T, D_MODEL), F32),
        compiler_params=_params(1),
        name="ffn",
    )(x, g, w_in_bf, w_in_bf, w_out_bf, g_final)


def _inproj_kernel(x_ref, g_ref, w_ref, wgl_ref, wfl_ref, bg_ref, cw_ref, cb_ref,
                   xa_ref, conv_ref, gelu_ref, gate_ref, qt_ref, kt_ref, vt_ref, fl_ref,
                   ext_ref, *, tm, tiles_per_seq):
    @pl.when(pl.program_id(0) % tiles_per_seq == 0)
    def _():
        ext_ref[0:CONV_HALO, :] = jnp.zeros((CONV_HALO, LRU_W), F32)

    h = _rms(x_ref[...], g_ref[...]).astype(BF16)

    def seg(lo, width):
        return _dot(h, w_ref[:, lo:lo + width])

    ext_ref[CONV_HALO:CONV_HALO + tm, :] = seg(IN_XB, LRU_W)
    ext = ext_ref[...]
    y = cb_ref[...]
    for k in range(CONV_W):
        lag = CONV_W - 1 - k
        shifted = pltpu.roll(ext, lag, 0) if lag else ext
        y = y + shifted[CONV_HALO:, :] * cw_ref[k:k + 1, :]
    ext_ref[0:CONV_HALO, :] = ext[tm:tm + CONV_HALO, :]
    conv_ref[...] = y
    gelu_ref[...] = jax.nn.gelu(seg(IN_GB, LRU_W))
    gate_ref[...] = jax.nn.sigmoid(_dot(h, wgl_ref[...]) + bg_ref[...]).astype(BF16)
    xa_ref[...] = seg(IN_XA, POOL_W)

    qkv_t = seg(IN_Q, 3 * FOX_W).T
    qt_ref[...] = (qkv_t[:FOX_W] * (FOX_HD ** -0.5)).astype(BF16)
    kt_ref[...] = qkv_t[FOX_W:2 * FOX_W].astype(BF16)
    vt_ref[...] = qkv_t[2 * FOX_W:].astype(BF16)
    fl_ref[...] = _dot_nt(wfl_ref[...], h)


def _inproj(x, layer, g, w_in_bf, w_gl_bf, w_fl_t_bf, b_gate, conv_w, conv_b,
            *, seq, tm=256):
    T = x.shape[0]
    params = (g, w_in_bf, w_gl_bf, w_fl_t_bf, b_gate, conv_w, conv_b)
    w_front = pl.BlockSpec((None, D_MODEL, IN_FL), lambda i: (layer, 0, 0),
                           pipeline_mode=pl.Buffered(1))
    param_specs = [w_front if p is w_in_bf else _layer_spec(p, layer) for p in params]

    def rows(width):
        return pl.BlockSpec((tm, width), lambda i: (i, 0))

    def cols(height):
        return pl.BlockSpec((height, tm), lambda i: (0, i))

    out_shape = (
        jax.ShapeDtypeStruct((T, POOL_W), F32),
        jax.ShapeDtypeStruct((T, LRU_W), F32),
        jax.ShapeDtypeStruct((T, LRU_W), F32),
        jax.ShapeDtypeStruct((T, N_BRANCH * D_MODEL), BF16),
        jax.ShapeDtypeStruct((FOX_W, T), BF16),
        jax.ShapeDtypeStruct((FOX_W, T), BF16),
        jax.ShapeDtypeStruct((FOX_W, T), BF16),
        jax.ShapeDtypeStruct((FOX_HEADS, T), F32),
    )
    out_specs = (
        rows(POOL_W), rows(LRU_W), rows(LRU_W), rows(N_BRANCH * D_MODEL),
        cols(FOX_W), cols(FOX_W), cols(FOX_W), cols(FOX_HEADS),
    )
    return pl.pallas_call(
        functools.partial(_inproj_kernel, tm=tm, tiles_per_seq=seq // tm),
        grid=(T // tm,),
        in_specs=[rows(D_MODEL)] + param_specs,
        out_specs=out_specs,
        out_shape=out_shape,
        scratch_shapes=[pltpu.VMEM((tm + CONV_HALO, LRU_W), F32)],
        compiler_params=_params(1),
        name="inproj",
    )(x, *params)


def _pool_kernel(xa_ref, wp_ref, sc_ref, o_ref, ext_ref, *, ts):
    s = pl.program_id(1)

    @pl.when(s == 0)
    def _():
        ext_ref[0:POOL_HALO, :] = jnp.zeros((POOL_HALO, POOL_W), F32)

    x = xa_ref[...]
    ext_ref[POOL_HALO:POOL_HALO + ts, :] = x
    e = ext_ref[...]
    ext_ref[0:POOL_HALO, :] = e[ts:ts + POOL_HALO, :]

    sums = []
    cur = e
    shift = 1
    for g in range(POOL_GROUPS):
        cur = cur + pltpu.roll(cur, shift, 0)
        sums.append(cur[POOL_HALO:, 0:POOL_GW])
        if g + 1 < POOL_GROUPS:
            cur = cur[:, POOL_GW:]
        shift *= 2

    pos = (s * ts + 1 + lax.broadcasted_iota(jnp.int32, (ts, 1), 0)).astype(F32)
    for g, w in enumerate(POOL_WINDOWS):
        inv_cnt = 1.0 / jnp.minimum(pos, float(w))
        mean = sums[g] * inv_cnt
        d = (mean - x[:, g * POOL_GW:(g + 1) * POOL_GW]).astype(BF16)
        y = _dot(d, wp_ref[g]) * sc_ref[:, g * POOL_GW:(g + 1) * POOL_GW]
        o_ref[:, g * POOL_GW:(g + 1) * POOL_GW] = y.astype(BF16)


def _pool(xa, layer, w_pool_bf, scale, *, ts=1024):
    B, S, _ = xa.shape
    return pl.pallas_call(
        functools.partial(_pool_kernel, ts=ts),
        grid=(B, S // ts),
        in_specs=[
            pl.BlockSpec((None, ts, POOL_W), lambda b, s: (b, s, 0)),
            _layer_spec(w_pool_bf, layer), _layer_spec(scale, layer),
        ],
        out_specs=pl.BlockSpec((None, ts, POOL_W), lambda b, s: (b, s, 0)),
        out_shape=jax.ShapeDtypeStruct((B, S, POOL_W), BF16),
        scratch_shapes=[pltpu.VMEM((ts + POOL_HALO, POOL_W), F32)],
        compiler_params=_params(2),
        name="pool",
    )(xa, w_pool_bf, scale)


def _rglru_kernel(y_ref, gelu_ref, wg_ref, ba_ref, bx_ref, lam_ref, o_ref,
                  a_ref, u_ref, hc_ref, *, ts, lane_chunk):
    @pl.when(pl.program_id(1) == 0)
    def _():
        hc_ref[...] = jnp.zeros((SUBLANES, LRU_W), F32)

    y = y_ref[...]
    yb = y.astype(BF16)
    for h in range(LRU_HEADS):
        cols = slice(h * LRU_HD, (h + 1) * LRU_HD)
        gates = _dot(yb[:, cols], wg_ref[h])
        r = jax.nn.sigmoid(gates[:, :LRU_HD] + ba_ref[:, cols])
        i = jax.nn.sigmoid(gates[:, LRU_HD:] + bx_ref[:, cols])
        log_a = -LRU_C * r * _softplus(-lam_ref[:, cols])
        a = jnp.exp(log_a)
        one_minus_a2 = -jnp.tanh(log_a) * (a * a + 1.0)
        a_ref[:, cols] = a
        u_ref[:, cols] = jnp.sqrt(one_minus_a2) * (i * y[:, cols])

    row = lax.broadcasted_iota(jnp.int32, (SUBLANES, lane_chunk), 0)
    for c in range(LRU_W // lane_chunk):
        cols = slice(c * lane_chunk, (c + 1) * lane_chunk)

        def group(gi, h_prev, cols=cols):
            off = pl.multiple_of(gi * SUBLANES, SUBLANES)
            a = a_ref[pl.ds(off, SUBLANES), cols]
            u = u_ref[pl.ds(off, SUBLANES), cols]
            for d in (1, 2, 4):
                a_sh = jnp.where(row >= d, pltpu.roll(a, d, 0), 1.0)
                u_sh = jnp.where(row >= d, pltpu.roll(u, d, 0), 0.0)
                u = a * u_sh + u
                a = a * a_sh
            hh = a * h_prev + u
            u_ref[pl.ds(off, SUBLANES), cols] = hh
            return jnp.broadcast_to(hh[SUBLANES - 1:SUBLANES, :], (SUBLANES, lane_chunk))

        hc_ref[:, cols] = lax.fori_loop(0, ts // SUBLANES, group, hc_ref[:, cols], unroll=4)

    o_ref[...] = (u_ref[...] * gelu_ref[...]).astype(BF16)


def _rglru(y, gelu_gb, layer, wg_bf, b_a, b_x, lam, *, ts=512, lane_chunk=1024):
    B, S, _ = y.shape
    tile = pl.BlockSpec((None, ts, LRU_W), lambda b, s: (b, s, 0))
    params = (wg_bf, b_a, b_x, lam)
    return pl.pallas_call(
        functools.partial(_rglru_kernel, ts=ts, lane_chunk=lane_chunk),
        grid=(B, S // ts),
        in_specs=[tile, tile] + [_layer_spec(p, layer) for p in params],
        out_specs=tile,
        out_shape=jax.ShapeDtypeStruct((B, S, LRU_W), BF16),
        scratch_shapes=[
            pltpu.VMEM((ts, LRU_W), F32),
            pltpu.VMEM((ts, LRU_W), F32),
            pltpu.VMEM((SUBLANES, LRU_W), F32),
        ],
        compiler_params=_params(2),
        name="rglru",
    )(y, gelu_gb, *params)


def _logf_cumsum_kernel(fl_ref, bf_ref, o_ref, *, seq):
    z = fl_ref[...] + bf_ref[...]
    x = jnp.minimum(z, 0.0) - jnp.log1p(jnp.exp(-jnp.abs(z)))
    lane = lax.broadcasted_iota(jnp.int32, x.shape, 1)
    d = 1
    while d < seq:
        x = x + jnp.where(lane >= d, pltpu.roll(x, d, 1), 0.0)
        d *= 2
    pieces = []
    rest = -x
    for _ in range(N_SPLIT):
        piece = rest.astype(BF16).astype(F32)
        pieces.append(piece)
        rest = rest - piece
    pad = jnp.zeros((BF16_ROWS - N_SPLIT, seq), F32)
    for h in range(FOX_HEADS):
        o_ref[h] = jnp.concatenate([p[h:h + 1] for p in pieces] + [pad], axis=0).astype(BF16)


def _logf_cumsum(fl_t, layer, b_f, *, seq):
    H, T = fl_t.shape
    return pl.pallas_call(
        functools.partial(_logf_cumsum_kernel, seq=seq),
        grid=(T // seq,),
        in_specs=[pl.BlockSpec((H, seq), lambda b: (0, b)), _layer_spec(b_f, layer)],
        out_specs=pl.BlockSpec((H, BF16_ROWS, seq), lambda b: (0, 0, b)),
        out_shape=jax.ShapeDtypeStruct((H, BF16_ROWS, T), BF16),
        compiler_params=_params(1),
        name="logf_cumsum",
    )(fl_t, b_f)


def _max_over_rows(s):
    while s.shape[0] % (2 * SUBLANES) == 0:
        half = s.shape[0] // 2
        s = jnp.maximum(s[:half], s[half:])
    return jnp.max(s, axis=0, keepdims=True)


def _fox_kernel(qt_ref, kt_ref, c_ref, vt_ref, o_ref, m_ref, acc_ref, s_even_ref, s_odd_ref,
                *, tq, tk, strip):
    qi = pl.program_id(2)
    n_strips = tq // strip

    def unit_rows(n_ones, width):
        row = lax.broadcasted_iota(jnp.int32, (BF16_ROWS, width), 0)
        return jnp.where(row < n_ones, 1.0, 0.0).astype(BF16)

    q_aug = [jnp.concatenate([qt_ref[:, c * strip:(c + 1) * strip], unit_rows(N_SPLIT, strip)],
                             axis=0) for c in range(n_strips)]

    def keys(off, width):
        return jnp.concatenate([kt_ref[:, pl.ds(off, width)], c_ref[:, pl.ds(off, width)]],
                               axis=0)

    def scores(k_aug, c):
        return _dot_tn(k_aug, q_aug[c])

    def update(off, width, c, s, first_query=None):
        if first_query is not None:
            key = lax.broadcasted_iota(jnp.int32, (width, strip), 0)
            qry = lax.broadcasted_iota(jnp.int32, (width, strip), 1) + first_query
            s = jnp.where(key <= qry, s, NEG_BIG)
        m = m_ref[c]
        m_new = jnp.maximum(m, _max_over_rows(s))
        alpha = jnp.exp(m - m_new)
        p = jnp.exp(s - m_new).astype(BF16)
        m_ref[c] = m_new
        v_aug = jnp.concatenate([vt_ref[:, pl.ds(off, width)], unit_rows(1, width)], axis=0)
        acc_ref[c] = alpha * acc_ref[c] + _dot(v_aug, p)

    for c in range(n_strips):
        m_ref[c] = jnp.full((1, strip), -jnp.inf, F32)
        acc_ref[c] = jnp.zeros((AUG_H, strip), F32)

    n_bulk = qi * (tq // tk)
    early = range(0, N_EARLY)
    late = range(N_EARLY, n_strips)

    def late_update(j, s_ref):
        off = pl.multiple_of(j * tk, tk)
        for c in late:
            update(off, tk, c, s_ref[c - late.start])

    def stage(j, read_ref, write_ref):
        off = pl.multiple_of(j * tk, tk)
        k_aug = keys(off, tk)
        s_early = [scores(k_aug, c) for c in early]
        if read_ref is not None:
            late_update(j - 1, read_ref)
        for c in late:
            write_ref[c - late.start] = scores(k_aug, c)
        for n, c in enumerate(early):
            update(off, tk, c, s_early[n])

    @pl.when(n_bulk > 0)
    def _():
        stage(0, None, s_even_ref)

    def two_stages(jj, carry):
        stage(2 * jj + 1, s_even_ref, s_odd_ref)
        stage(2 * jj + 2, s_odd_ref, s_even_ref)
        return carry

    lax.fori_loop(0, (n_bulk - 1) // 2, two_stages, 0)
    last = n_bulk - 1
    last_is_odd = last % 2 == 1

    def tail(pending_ref):
        base = pl.multiple_of(qi * tq, tq)
        k_diag = keys(base, tq)
        s_diag = [scores(k_diag[:, :(c + 1) * strip], c) for c in range(n_strips)]
        if pending_ref is not None:
            late_update(last, pending_ref)
        for c in range(n_strips):
            update(base, (c + 1) * strip, c, s_diag[c], first_query=c * strip)
            acc = acc_ref[c]
            o_ref[:, c * strip:(c + 1) * strip] = (
                acc[:FOX_HD] / acc[FOX_HD:FOX_HD + 1]).astype(BF16)

    @pl.when(n_bulk == 0)
    def _():
        tail(None)

    @pl.when(jnp.logical_and(n_bulk > 0, last_is_odd))
    def _():
        stage(last, s_even_ref, s_odd_ref)
        tail(s_odd_ref)

    @pl.when(jnp.logical_and(n_bulk > 0, jnp.logical_not(last_is_odd)))
    def _():
        tail(s_even_ref)


def _fox(q_t, k_t, c_pieces, v_t, *, batch, seq, tq=1024, tk=1024, strip=256):
    B, S = batch, seq
    nq = S // tq
    n_strips = tq // strip
    whole_seq = pl.BlockSpec((FOX_HD, S), lambda b, h, i: (h, b))
    q_tile = pl.BlockSpec((FOX_HD, tq), lambda b, h, i: (h, b * nq + i))
    return pl.pallas_call(
        functools.partial(_fox_kernel, tq=tq, tk=tk, strip=strip),
        grid=(B, FOX_HEADS, nq),
        in_specs=[
            q_tile, whole_seq,
            pl.BlockSpec((None, BF16_ROWS, S), lambda b, h, i: (h, 0, b)),
            whole_seq,
        ],
        out_specs=q_tile,
        out_shape=jax.ShapeDtypeStruct((FOX_W, B * S), BF16),
        scratch_shapes=[
            pltpu.VMEM((n_strips, 1, strip), F32),
            pltpu.VMEM((n_strips, AUG_H, strip), F32),
            pltpu.VMEM((n_strips - N_EARLY, tk, strip), F32),
            pltpu.VMEM((n_strips - N_EARLY, tk, strip), F32),
        ],
        compiler_params=_params(3),
        name="fox_attention",
    )(q_t, k_t, c_pieces, v_t)


def _merge_cross_kernel(x_ref, a_ref, b_ref, ct_ref, gate_ref, mk_ref, mv_ref,
                        ua_ref, ub_ref, uc_ref, wo_ref, g_ref, wq_ref, wxo_ref, o_ref):
    ups = (_dot(a_ref[...], ua_ref[...]), _dot(b_ref[...], ub_ref[...]),
           _dot_tn(ct_ref[...], uc_ref[...]))
    merged = None
    for n, up in enumerate(ups):
        term = gate_ref[:, n * D_MODEL:(n + 1) * D_MODEL].astype(F32) * up
        merged = term if merged is None else merged + term
    x = x_ref[...] + _dot(merged.astype(BF16), wo_ref[...])

    h = _rms(x, g_ref[...]).astype(BF16)
    q = (_dot(h, wq_ref[...]) * (X_HD ** -0.5)).astype(BF16)
    heads = []
    for hd in range(X_HEADS):
        cols = slice(hd * X_HD, (hd + 1) * X_HD)
        s = _dot_nt(q[:, cols], mk_ref[:, cols])
        e = jnp.exp(s - jnp.max(s, axis=-1, keepdims=True))
        p = e / jnp.sum(e, axis=-1, keepdims=True)
        heads.append(_dot(p.astype(BF16), mv_ref[:, cols]).astype(BF16))
    o_ref[...] = x + _dot(jnp.concatenate(heads, axis=-1), wxo_ref[...])


def _merge_cross(x, ya, yb, yc_t, gate, mk, mv, layer, ua, ub, uc, wo, g_cross, wq, wxo,
                 *, seq, tm=1024):
    T = x.shape[0]
    tiles_per_seq = seq // tm

    def rows(width):
        return pl.BlockSpec((tm, width), lambda i: (i, 0))

    mem_kv = pl.BlockSpec((None, N_MEM, D_MODEL), lambda i: (i // tiles_per_seq, 0, 0))
    params = (ua, ub, uc, wo, g_cross, wq, wxo)
    return pl.pallas_call(
        _merge_cross_kernel,
        grid=(T // tm,),
        in_specs=[
            rows(D_MODEL), rows(POOL_W), rows(LRU_W),
            pl.BlockSpec((FOX_W, tm), lambda i: (0, i)),
            rows(N_BRANCH * D_MODEL), mem_kv, mem_kv,
        ] + [_layer_spec(p, layer) for p in params],
        out_specs=rows(D_MODEL),
        out_shape=jax.ShapeDtypeStruct((T, D_MODEL), F32),
        compiler_params=_params(1),
        name="merge_cross",
    )(x, ya, yb, yc_t, gate, mk, mv, *params)


def _memkv_kernel(m_ref, g_ref, w_ref, k_ref, v_ref):
    h = _rms(m_ref[...], g_ref[...]).astype(BF16)
    k_ref[...] = _dot(h, w_ref[:, :D_MODEL]).astype(BF16)
    v_ref[...] = _dot(h, w_ref[:, D_MODEL:]).astype(BF16)


def _memkv(mem, layer, g, w_kv_bf):
    B = mem.shape[0]
    tile = pl.BlockSpec((None, N_MEM, D_MODEL), lambda b: (b, 0, 0))
    return pl.pallas_call(
        _memkv_kernel,
        grid=(B,),
        in_specs=[tile, _layer_spec(g, layer), _layer_spec(w_kv_bf, layer)],
        out_specs=(tile, tile),
        out_shape=(jax.ShapeDtypeStruct((B, N_MEM, D_MODEL), BF16),) * 2,
        compiler_params=_params(1),
        name="mem_kv",
    )(mem, g, w_kv_bf)


def kernel(x, mem, g_ffn1, w_ffn1_in, w_ffn1_out, g_mix, w_in, b_f, b_gate, w_pool, pool_scale,
           w_up_a, conv_w, conv_b, w_rg_a, b_rg_a, w_rg_x, b_rg_x, lru_lambda, w_up_b, w_up_c, w_o,
           g_cross, g_mem, w_xq, w_xkv, w_xo, g_ffn2, w_ffn2_in, w_ffn2_out, g_final):
    B, S, D = x.shape
    T = B * S
    depth = g_ffn1.shape[0]

    def vec(p):
        return p.reshape(p.shape[0], 1, p.shape[1])

    def bf(w):
        return w.astype(BF16)

    g_ffn1, g_mix, g_cross, g_mem, g_ffn2 = map(vec, (g_ffn1, g_mix, g_cross, g_mem, g_ffn2))
    b_gate, pool_scale, conv_b, b_rg_a, b_rg_x, lru_lambda = map(
        vec, (b_gate, pool_scale, conv_b, b_rg_a, b_rg_x, lru_lambda))
    g_fin = g_final.reshape(1, 1, D)
    b_f_col = b_f.reshape(depth, FOX_HEADS, 1)
    w_ffn1_in, w_ffn1_out, w_ffn2_in, w_ffn2_out = map(
        bf, (w_ffn1_in, w_ffn1_out, w_ffn2_in, w_ffn2_out))
    w_pool, w_up_a, w_up_b, w_up_c, w_o, w_xq, w_xkv, w_xo = map(
        bf, (w_pool, w_up_a, w_up_b, w_up_c, w_o, w_xq, w_xkv, w_xo))
    w_in = bf(w_in)
    w_gl = w_in[:, :, IN_GL:]
    w_fl_t = jnp.swapaxes(w_in[:, :, IN_FL:IN_GL], 1, 2)
    w_gates = bf(jnp.concatenate([w_rg_a, w_rg_x], axis=-1))

    xt = x.reshape(T, D)
    for l in range(depth):
        xt = _ffn(xt, l, g_ffn1, w_ffn1_in, w_ffn1_out, g_fin, final_norm=False)

        xa, xb_conv, gb_gelu, gate, q_t, k_t, v_t, fl_t = _inproj(
            xt, l, g_mix, w_in, w_gl, w_fl_t, b_gate, conv_w, conv_b, seq=S)
        ya = _pool(xa.reshape(B, S, POOL_W), l, w_pool, pool_scale)
        yb = _rglru(xb_conv.reshape(B, S, LRU_W), gb_gelu.reshape(B, S, LRU_W), l,
                    w_gates, b_rg_a, b_rg_x, lru_lambda)
        c_pieces = _logf_cumsum(fl_t, l, b_f_col, seq=S)
        yc_t = _fox(q_t, k_t, c_pieces, v_t, batch=B, seq=S)
        mk, mv = _memkv(mem, l, g_mem, w_xkv)
        xt = _merge_cross(xt, ya.reshape(T, POOL_W), yb.reshape(T, LRU_W), yc_t, gate, mk, mv, l,
                          w_up_a, w_up_b, w_up_c, w_o, g_cross, w_xq, w_xo, seq=S)

        xt = _ffn(xt, l, g_ffn2, w_ffn2_in, w_ffn2_out, g_fin, final_norm=(l == depth - 1))
    return xt.reshape(B, S, D)
```
